```python
import math
import jax, jax.numpy as jnp
from jax import lax
import numpy as np

D_MODEL = 4096
BATCH = 2
SEQ = 4096
DEPTH = 4

N_MIXERS = 2
BLOCK = 128
SB_HEAD_DIM = 128
SB_HEADS = D_MODEL // SB_HEAD_DIM
SWA_HEAD_DIM = 64
SWA_HEADS = D_MODEL // SWA_HEAD_DIM
SWA_KV_HEADS = SWA_HEADS // 8
SWA_GROUP = SWA_HEADS // SWA_KV_HEADS
SWA_QKV_DIM = (SWA_HEADS + 2 * SWA_KV_HEADS) * SWA_HEAD_DIM
WINDOW = 128
D_FF = (7 * D_MODEL) // 2
CONV_WIDTH = 3
RMS_EPS = 1e-6
N_SB_LAYERS = (DEPTH + 1) // N_MIXERS
N_SWA_LAYERS = DEPTH // N_MIXERS

kernel_name = "stickbreak_swa_sink_alibi_convffn_hybrid"


def rms_norm(x, gain):
    x32 = x.astype(jnp.float32)
    y = x32 * lax.rsqrt(jnp.mean(x32 * x32, axis=-1, keepdims=True) + RMS_EPS)
    return (y * gain.astype(jnp.float32)).astype(x.dtype)


def alibi_slopes(n_heads):
    return jnp.asarray(2.0 ** (-8.0 * np.arange(1, n_heads + 1) / n_heads), jnp.float32)


def stick_breaking_attention(xn, w_qkv, w_o):
    B, S, _ = xn.shape
    q, k, v = jnp.split(xn @ w_qkv, 3, axis=-1)

    def to_heads(t):
        return t.reshape(B, S, SB_HEADS, SB_HEAD_DIM).transpose(0, 2, 1, 3)

    q, k, v = to_heads(q), to_heads(k), to_heads(v)
    scale = SB_HEAD_DIM ** -0.5
    outs = []
    for i in range(S // BLOCK):
        n_keys = (i + 1) * BLOCK
        q_blk = q[:, :, i * BLOCK:(i + 1) * BLOCK]
        k_pre = k[:, :, :n_keys]
        v_pre = v[:, :, :n_keys]
        z = jnp.einsum('bhqd,bhkd->bhqk', q_blk, k_pre).astype(jnp.float32) * scale
        t_pos = i * BLOCK + jnp.arange(BLOCK)
        s_pos = jnp.arange(n_keys)
        strict = s_pos[None, :] < t_pos[:, None]
        log_stay = jnp.where(strict, jax.nn.log_sigmoid(-z), 0.0)
        later = lax.cumsum(log_stay, axis=3, reverse=True) - log_stay
        weights = jnp.where(strict, jnp.exp(jax.nn.log_sigmoid(z) + later), 0.0)
        outs.append(jnp.einsum('bhqk,bhkd->bhqd', weights.astype(v.dtype), v_pre))
    o = jnp.concatenate(outs, axis=2).transpose(0, 2, 1, 3).reshape(B, S, SB_HEADS * SB_HEAD_DIM)
    return o @ w_o


def sliding_window_attention(xn, w_qkv, w_o, sinks):
    B, S, _ = xn.shape
    nb = S // BLOCK
    qd = SWA_HEADS * SWA_HEAD_DIM
    kd = SWA_KV_HEADS * SWA_HEAD_DIM
    qkv = xn @ w_qkv
    q = qkv[..., :qd].reshape(B, nb, BLOCK, SWA_KV_HEADS, SWA_GROUP, SWA_HEAD_DIM)
    k = qkv[..., qd:qd + kd]
    v = qkv[..., qd + kd:]

    def band(t):
        cur = t.reshape(B, nb, BLOCK, SWA_KV_HEADS, SWA_HEAD_DIM)
        prev = jnp.pad(cur, ((0, 0), (1, 0), (0, 0), (0, 0), (0, 0)))[:, :-1]
        return jnp.concatenate([prev, cur], axis=2).swapaxes(0, 1)

    k_band, v_band = band(k), band(v)
    q_blk = q.swapaxes(0, 1)
    q_idx = jnp.arange(BLOCK)[:, None]
    k_idx = jnp.arange(2 * BLOCK)[None, :]
    dist = q_idx + BLOCK - k_idx
    in_window = (dist >= 0) & (dist < WINDOW)
    slopes = alibi_slopes(SWA_HEADS).reshape(SWA_KV_HEADS, SWA_GROUP)
    alibi = -slopes[:, :, None, None] * dist.astype(jnp.float32)
    sink_logits = sinks.astype(jnp.float32).reshape(SWA_KV_HEADS, SWA_GROUP)
    scale = SWA_HEAD_DIM ** -0.5

    def attend_block(args):
        qb, kb, vb, blk = args
        s = jnp.einsum('bqhgd,bkhd->bhgqk', qb, kb).astype(jnp.float32) * scale + alibi
        valid = in_window & ((blk * BLOCK + k_idx - BLOCK) >= 0)
        s = jnp.where(valid, s, -jnp.inf)
        sink = jnp.broadcast_to(sink_logits[None, :, :, None, None], s.shape[:-1] + (1,))
        p = jax.nn.softmax(jnp.concatenate([s, sink], axis=-1), axis=-1)[..., :-1]
        return jnp.einsum('bhgqk,bkhd->bqhgd', p.astype(vb.dtype), vb)

    o = lax.map(attend_block, (q_blk, k_band, v_band, jnp.arange(nb)))
    o = o.swapaxes(0, 1).reshape(B, S, qd)
    return o @ w_o


def conv_ffn(xn, w_in, conv_w, conv_b, w_down):
    S = xn.shape[1]
    h = xn @ w_in
    hp = jnp.pad(h, ((0, 0), (CONV_WIDTH - 1, 0), (0, 0)))
    y = conv_b
    for j in range(CONV_WIDTH):
        y = y + conv_w[j] * hp[:, j:j + S]
    gate, up = jnp.split(y, 2, axis=-1)
    return (jax.nn.silu(gate) * up) @ w_down


def setup_inputs(seed: int = 0) -> dict:
    key = jax.random.key(seed)
    ks = jax.random.split(key, 13)
    f32 = jnp.float32
    x = jax.random.normal(ks[0], (BATCH, SEQ, D_MODEL), f32)
    attn_norm = 1.0 + 0.02 * jax.random.normal(ks[1], (DEPTH, D_MODEL), f32)
    ffn_norm = 1.0 + 0.02 * jax.random.normal(ks[2], (DEPTH, D_MODEL), f32)
    sb_w_qkv = jax.random.normal(ks[3], (N_SB_LAYERS, D_MODEL, 3 * SB_HEADS * SB_HEAD_DIM), f32) * D_MODEL ** -0.5
    sb_w_o = jax.random.normal(ks[4], (N_SB_LAYERS, SB_HEADS * SB_HEAD_DIM, D_MODEL), f32) * (SB_HEADS * SB_HEAD_DIM) ** -0.5
    swa_w_qkv = jax.random.normal(ks[5], (N_SWA_LAYERS, D_MODEL, SWA_QKV_DIM), f32) * D_MODEL ** -0.5
    swa_w_o = jax.random.normal(ks[6], (N_SWA_LAYERS, SWA_HEADS * SWA_HEAD_DIM, D_MODEL), f32) * (SWA_HEADS * SWA_HEAD_DIM) ** -0.5
    swa_sinks = 0.5 * jax.random.normal(ks[7], (N_SWA_LAYERS, SWA_HEADS), f32)
    ffn_w_in = jax.random.normal(ks[8], (DEPTH, D_MODEL, 2 * D_FF), f32) * D_MODEL ** -0.5
    ffn_conv_w = jax.random.normal(ks[9], (DEPTH, CONV_WIDTH, 2 * D_FF), f32) * CONV_WIDTH ** -0.5
    ffn_conv_b = 0.02 * jax.random.normal(ks[10], (DEPTH, 2 * D_FF), f32)
    ffn_w_down = jax.random.normal(ks[11], (DEPTH, D_FF, D_MODEL), f32) * D_FF ** -0.5
    final_norm = 1.0 + 0.02 * jax.random.normal(ks[12], (D_MODEL,), f32)
    return {"x": x, "attn_norm": attn_norm, "ffn_norm": ffn_norm,
            "sb_w_qkv": sb_w_qkv, "sb_w_o": sb_w_o,
            "swa_w_qkv": swa_w_qkv, "swa_w_o": swa_w_o, "swa_sinks": swa_sinks,
            "ffn_w_in": ffn_w_in, "ffn_conv_w": ffn_conv_w, "ffn_conv_b": ffn_conv_b,
            "ffn_w_down": ffn_w_down, "final_norm": final_norm}


def reference(x, attn_norm, ffn_norm, sb_w_qkv, sb_w_o, swa_w_qkv, swa_w_o, swa_sinks,
              ffn_w_in, ffn_conv_w, ffn_conv_b, ffn_w_down, final_norm):
    h = x
    for i in range(DEPTH):
        xn = rms_norm(h, attn_norm[i])
        j = i // N_MIXERS
        if i % N_MIXERS == 0:
            mix = stick_breaking_attention(xn, sb_w_qkv[j], sb_w_o[j])
        else:
            mix = sliding_window_attention(xn, swa_w_qkv[j], swa_w_o[j], swa_sinks[j])
        h = h + mix
        h = h + conv_ffn(rms_norm(h, ffn_norm[i]), ffn_w_in[i], ffn_conv_w[i],
                         ffn_conv_b[i], ffn_w_down[i])
    return rms_norm(h, final_norm)
```

```python
import functools

import jax
import jax.numpy as jnp
import numpy as np
from jax import lax
from jax.experimental import pallas as pl
from jax.experimental.pallas import tpu as pltpu

F32 = jnp.float32
BF16 = jnp.bfloat16

RMS_EPS = 1e-6
SB_HEAD_DIM = 128
SWA_HEAD_DIM = 64
SWA_GROUP = 8
SWA_BLOCK = 128
CONV_WIDTH = 3
NEG_BIG = -1e30

VMEM_LIMIT = 56 * 1024 * 1024


def _params(*sem):
    return pltpu.CompilerParams(dimension_semantics=sem, vmem_limit_bytes=VMEM_LIMIT)


def _rmsnorm_kernel(x_ref, g_ref, o_ref):
    x = x_ref[...]
    ms = jnp.mean(x * x, axis=-1, keepdims=True)
    o_ref[...] = ((x * lax.rsqrt(ms + RMS_EPS)) * g_ref[...]).astype(o_ref.dtype)


def rmsnorm(x, gain, layer, out_dtype, tm=256):
    m, d = x.shape
    gain3 = gain.reshape(-1, 1, d)
    return pl.pallas_call(
        _rmsnorm_kernel,
        grid=(m // tm,),
        in_specs=[pl.BlockSpec((tm, d), lambda i: (i, 0)),
                  pl.BlockSpec((None, 1, d), lambda i: (layer, 0, 0))],
        out_specs=pl.BlockSpec((tm, d), lambda i: (i, 0)),
        out_shape=jax.ShapeDtypeStruct((m, d), out_dtype),
        compiler_params=_params("arbitrary"),
        name="rmsnorm",
    )(x, gain3)


def _proj_kernel(x_ref, w_ref, *rest, has_res):
    if has_res:
        r_ref, o_ref, wb_ref = rest
    else:
        o_ref, wb_ref = rest

    @pl.when(pl.program_id(1) == 0)
    def _():
        wb_ref[...] = w_ref[...].astype(BF16)

    acc = jnp.dot(x_ref[...], wb_ref[...], preferred_element_type=F32)
    if has_res:
        acc = acc + r_ref[...]
    o_ref[...] = acc.astype(o_ref.dtype)


def proj(x, w, layer, out_dtype, residual=None, tm=1024, tn=512):
    m, k = x.shape
    n = w.shape[-1]
    in_specs = [pl.BlockSpec((tm, k), lambda j, i: (i, 0)),
                pl.BlockSpec((None, k, tn), lambda j, i: (layer, 0, j))]
    args = [x, w]
    if residual is not None:
        in_specs.append(pl.BlockSpec((tm, tn), lambda j, i: (i, j)))
        args.append(residual)
    return pl.pallas_call(
        functools.partial(_proj_kernel, has_res=residual is not None),
        grid=(n // tn, m // tm),
        in_specs=in_specs,
        out_specs=pl.BlockSpec((tm, tn), lambda j, i: (i, j)),
        out_shape=jax.ShapeDtypeStruct((m, n), out_dtype),
        scratch_shapes=[pltpu.VMEM((k, tn), BF16)],
        compiler_params=_params("arbitrary", "arbitrary"),
        name="proj",
    )(*args)


def _ffn_in_kernel(x_ref, wg_ref, wu_ref, cwg_ref, cwu_ref, bg_ref, bu_ref, o_ref,
                   wb_ref, hbuf_ref, *, tm, tn, tiles_per_seq, chunk):
    i = pl.program_id(1)

    @pl.when(i == 0)
    def _():
        wb_ref[:, :tn] = wg_ref[...].astype(BF16)
        wb_ref[:, tn:] = wu_ref[...].astype(BF16)

    @pl.when(i % tiles_per_seq == 0)
    def _():
        hbuf_ref[0:8, :] = jnp.zeros((8, 2 * tn), F32)

    hbuf_ref[8:8 + tm, :] = jnp.dot(x_ref[...], wb_ref[...], preferred_element_type=F32)

    cw = jnp.concatenate([cwg_ref[...], cwu_ref[...]], axis=1)
    bias = jnp.concatenate([bg_ref[...], bu_ref[...]], axis=1)
    w0, w1, w2 = cw[0:1], cw[1:2], cw[2:3]

    def body(c, carry):
        r = pl.multiple_of(c * chunk, chunk)
        big = hbuf_ref[pl.ds(r, chunk + 8), :]
        h0 = big[8:]
        h1 = pltpu.roll(big, 1, 0)[8:]
        h2 = pltpu.roll(big, 2, 0)[8:]
        y = ((bias + w0 * h2) + w1 * h1) + w2 * h0
        gate, up = y[:, :tn], y[:, tn:]
        act = (gate * jax.nn.sigmoid(gate)) * up
        o_ref[pl.ds(r, chunk), :] = act.astype(o_ref.dtype)
        return carry

    lax.fori_loop(0, tm // chunk, body, 0)
    hbuf_ref[0:8, :] = hbuf_ref[tm:tm + 8, :]


def ffn_in(x, w_in, conv_w, conv_b, layer, seq, tm=1024, tn=256, chunk=64):
    m, k = x.shape
    d_ff = w_in.shape[-1] // 2
    nj = d_ff // tn
    conv_b3 = conv_b.reshape(conv_b.shape[0], 1, 2 * d_ff)
    return pl.pallas_call(
        functools.partial(_ffn_in_kernel, tm=tm, tn=tn, tiles_per_seq=seq // tm, chunk=chunk),
        grid=(nj, m // tm),
        in_specs=[pl.BlockSpec((tm, k), lambda j, i: (i, 0)),
                  pl.BlockSpec((None, k, tn), lambda j, i: (layer, 0, j)),
                  pl.BlockSpec((None, k, tn), lambda j, i: (layer, 0, j + nj)),
                  pl.BlockSpec((None, CONV_WIDTH, tn), lambda j, i: (layer, 0, j)),
                  pl.BlockSpec((None, CONV_WIDTH, tn), lambda j, i: (layer, 0, j + nj)),
                  pl.BlockSpec((None, 1, tn), lambda j, i: (layer, 0, j)),
                  pl.BlockSpec((None, 1, tn), lambda j, i: (layer, 0, j + nj))],
        out_specs=pl.BlockSpec((tm, tn), lambda j, i: (i, j)),
        out_shape=jax.ShapeDtypeStruct((m, d_ff), BF16),
        scratch_shapes=[pltpu.VMEM((k, 2 * tn), BF16),
                        pltpu.VMEM((tm + 8, 2 * tn), F32)],
        compiler_params=_params("arbitrary", "arbitrary"),
        name="ffn_in",
    )(x, w_in, w_in, conv_w, conv_w, conv_b3, conv_b3)


def _ffn_down_kernel(x_ref, w_ref, r_ref, o_ref):
    @pl.when(pl.program_id(2) == 0)
    def _():
        o_ref[...] = r_ref[...]

    o_ref[...] += jnp.dot(x_ref[...], w_ref[...].astype(BF16), preferred_element_type=F32)


def ffn_down(x, w, layer, residual, tm=2048, tn=1024, tk=1024):
    m, k = x.shape
    n = w.shape[-1]
    return pl.pallas_call(
        _ffn_down_kernel,
        grid=(m // tm, n // tn, k // tk),
        in_specs=[pl.BlockSpec((tm, tk), lambda i, j, kk: (i, kk)),
                  pl.BlockSpec((None, tk, tn), lambda i, j, kk: (layer, kk, j)),
                  pl.BlockSpec((tm, tn), lambda i, j, kk: (i, j))],
        out_specs=pl.BlockSpec((tm, tn), lambda i, j, kk: (i, j)),
        out_shape=jax.ShapeDtypeStruct((m, n), F32),
        compiler_params=_params("arbitrary", "arbitrary", "arbitrary"),
        name="ffn_down",
    )(x, w, residual)


def _sb_kernel(q_ref, k_ref, v_ref, tri_ref, o_ref, *, seq, tile, scale):
    t = tile
    tri = tri_ref[...]
    row = lax.broadcasted_iota(jnp.int32, (t, t), 0)
    col = lax.broadcasted_iota(jnp.int32, (t, t), 1)
    strict = col < row

    def visit(q, kb, masked, c, acc):
        ks = pl.multiple_of(kb * t, t)
        k = k_ref[pl.ds(ks, t), :]
        v = v_ref[pl.ds(ks, t), :]
        z = lax.dot_general(q, k, (((1,), (1,)), ((), ())),
                            preferred_element_type=F32) * scale
        sp = jnp.maximum(z, 0.0) + jnp.log(1.0 + jnp.exp(-jnp.abs(z)))
        if masked:
            sp = jnp.where(strict, sp, 0.0)
        hi = sp.astype(BF16)
        lo = (sp - hi.astype(F32)).astype(BF16)
        cum = (jnp.dot(hi, tri, preferred_element_type=F32)
               + jnp.dot(lo, tri, preferred_element_type=F32))
        w = jnp.exp(((z - sp) - cum) - c)
        if masked:
            w = jnp.where(strict, w, 0.0)
        acc = acc + jnp.dot(w.astype(BF16), v, preferred_element_type=F32)
        c = c + (cum[:, 0:1] + sp[:, 0:1])
        return c, acc

    def q_body(qi, carry):
        qs = pl.multiple_of(qi * t, t)
        q = q_ref[pl.ds(qs, t), :]
        c = jnp.zeros((t, 1), F32)
        acc = jnp.zeros((t, SB_HEAD_DIM), F32)
        c, acc = visit(q, qi, True, c, acc)

        def k_body(n, ca):
            return visit(q, qi - 1 - n, False, ca[0], ca[1])

        c, acc = lax.fori_loop(0, qi, k_body, (c, acc))
        o_ref[pl.ds(qs, t), :] = acc.astype(o_ref.dtype)
        return carry

    lax.fori_loop(0, seq // t, q_body, 0)


def sb_attention(qkv, batch, seq, tile=256):
    m = qkv.shape[0]
    heads = qkv.shape[1] // (3 * SB_HEAD_DIM)
    idx = jnp.arange(tile)
    tri = (idx[:, None] > idx[None, :]).astype(BF16)
    return pl.pallas_call(
        functools.partial(_sb_kernel, seq=seq, tile=tile, scale=SB_HEAD_DIM ** -0.5),
        grid=(batch, heads),
        in_specs=[pl.BlockSpec((seq, SB_HEAD_DIM), lambda b, h: (b, h)),
                  pl.BlockSpec((seq, SB_HEAD_DIM), lambda b, h: (b, heads + h)),
                  pl.BlockSpec((seq, SB_HEAD_DIM), lambda b, h: (b, 2 * heads + h)),
                  pl.BlockSpec((tile, tile), lambda b, h: (0, 0))],
        out_specs=pl.BlockSpec((seq, SB_HEAD_DIM), lambda b, h: (b, h)),
        out_shape=jax.ShapeDtypeStruct((m, heads * SB_HEAD_DIM), BF16),
        compiler_params=_params("arbitrary", "arbitrary"),
        name="sb_attention",
    )(qkv, qkv, qkv, tri)


def _swa_kernel(slopes_ref, sinks_ref, q_ref, k_ref, v_ref, o_ref, *, seq, scale):
    blk_sz = SWA_BLOCK
    kvh = pl.program_id(1)
    odd = (kvh % 2) == 1
    lane = lax.broadcasted_iota(jnp.int32, (2 * blk_sz, 128), 1)
    low = lane < SWA_HEAD_DIM

    q_idx = lax.broadcasted_iota(jnp.int32, (blk_sz, 2 * blk_sz), 0)
    k_idx = lax.broadcasted_iota(jnp.int32, (blk_sz, 2 * blk_sz), 1)
    dist = q_idx + blk_sz - k_idx
    in_window = (dist >= 0) & (dist < blk_sz)
    dist_f = dist.astype(F32)

    def place(band):
        x = band.astype(F32)
        x = jnp.where(odd, pltpu.roll(x, SWA_HEAD_DIM, 1), x)
        at0 = jnp.where(low, x, 0.0)
        at64 = pltpu.roll(at0, SWA_HEAD_DIM, 1)
        return at0.astype(BF16), at64.astype(BF16)

    def body(blk, carry):
        cur = pl.multiple_of(blk * blk_sz, blk_sz)
        prev = pl.multiple_of(jnp.maximum(blk - 1, 0) * blk_sz, blk_sz)
        kband = jnp.concatenate([k_ref[pl.ds(prev, blk_sz), :], k_ref[pl.ds(cur, blk_sz), :]], axis=0)
        vband = jnp.concatenate([v_ref[pl.ds(prev, blk_sz), :], v_ref[pl.ds(cur, blk_sz), :]], axis=0)
        k_at = place(kband)
        v_at = place(vband)
        valid = in_window & ((blk * blk_sz + k_idx - blk_sz) >= 0)
        for c in range(SWA_GROUP // 2):
            qc = q_ref[pl.ds(cur, blk_sz), c * 128:(c + 1) * 128]
            out = None
            for half in range(2):
                head = kvh * SWA_GROUP + 2 * c + half
                s = lax.dot_general(qc, k_at[half], (((1,), (1,)), ((), ())),
                                    preferred_element_type=F32)
                s = s * scale - slopes_ref[head] * dist_f
                s = jnp.where(valid, s, NEG_BIG)
                sink = sinks_ref[head]
                mx = jnp.maximum(jnp.max(s, axis=-1, keepdims=True), sink)
                p = jnp.exp(s - mx)
                denom = jnp.sum(p, axis=-1, keepdims=True) + jnp.exp(sink - mx)
                p = p / denom
                o = jnp.dot(p.astype(BF16), v_at[half], preferred_element_type=F32)
                out = o if out is None else out + o
            o_ref[pl.ds(cur, blk_sz), c * 128:(c + 1) * 128] = out.astype(o_ref.dtype)
        return carry

    lax.fori_loop(0, seq // blk_sz, body, 0)


def swa_attention(qkv, slopes, sinks, batch, seq, heads):
    m = qkv.shape[0]
    kv_heads = heads // SWA_GROUP
    qd = heads * SWA_HEAD_DIM
    gw = SWA_GROUP * SWA_HEAD_DIM
    k_blk0 = qd // 128
    v_blk0 = (qd + kv_heads * SWA_HEAD_DIM) // 128
    smem = pl.BlockSpec(memory_space=pltpu.SMEM)
    return pl.pallas_call(
        functools.partial(_swa_kernel, seq=seq, scale=SWA_HEAD_DIM ** -0.5),
        grid=(batch, kv_heads),
        in_specs=[smem, smem,
                  pl.BlockSpec((seq, gw), lambda b, h: (b, h)),
                  pl.BlockSpec((seq, 128), lambda b, h: (b, k_blk0 + h // 2)),
                  pl.BlockSpec((seq, 128), lambda b, h: (b, v_blk0 + h // 2))],
        out_specs=pl.BlockSpec((seq, gw), lambda b, h: (b, h)),
        out_shape=jax.ShapeDtypeStruct((m, qd), BF16),
        compiler_params=_params("arbitrary", "arbitrary"),
        name="swa_attention",
    )(slopes, sinks, qkv, qkv, qkv)


def kernel(x, attn_norm, ffn_norm, sb_w_qkv, sb_w_o, swa_w_qkv, swa_w_o, swa_sinks,
           ffn_w_in, ffn_conv_w, ffn_conv_b, ffn_w_down, final_norm):
    batch, seq, d = x.shape
    depth = attn_norm.shape[0]
    swa_heads = swa_sinks.shape[-1]
    slopes = jnp.asarray(2.0 ** (-8.0 * np.arange(1, swa_heads + 1) / swa_heads), F32)
    h = x.reshape(batch * seq, d)
    for i in range(depth):
        xn = rmsnorm(h, attn_norm, i, BF16)
        j = i // 2
        if i % 2 == 0:
            qkv = proj(xn, sb_w_qkv, j, BF16)
            mix = sb_attention(qkv, batch, seq)
            h = proj(mix, sb_w_o, j, F32, residual=h)
        else:
            qkv = proj(xn, swa_w_qkv, j, BF16)
            mix = swa_attention(qkv, slopes, swa_sinks[j], batch, seq, swa_heads)
            h = proj(mix, swa_w_o, j, F32, residual=h)
        xn = rmsnorm(h, ffn_norm, i, BF16)
        act = ffn_in(xn, ffn_w_in, ffn_conv_w, ffn_conv_b, i, seq)
        h = ffn_down(act, ffn_w_down, i, h)
    out = rmsnorm(h, final_norm.reshape(1, d), 0, F32)
    return out.reshape(batch, seq, d)
```

```python
import functools

import jax
import jax.numpy as jnp
import numpy as np
from jax import lax
from jax.experimental import pallas as pl
from jax.experimental.pallas import tpu as pltpu

F32 = jnp.float32
BF16 = jnp.bfloat16

RMS_EPS = 1e-6
SB_HEAD_DIM = 128
SWA_HEAD_DIM = 64
SWA_GROUP = 8
SWA_BLOCK = 128
CONV_WIDTH = 3
NEG_BIG = -1e30

VMEM_LIMIT = 56 * 1024 * 1024


def _params(*sem):
    return pltpu.CompilerParams(dimension_semantics=sem, vmem_limit_bytes=VMEM_LIMIT)


def _rmsnorm_kernel(x_ref, g_ref, o_ref):
    x = x_ref[...]
    ms = jnp.mean(x * x, axis=-1, keepdims=True)
    o_ref[...] = ((x * lax.rsqrt(ms + RMS_EPS)) * g_ref[...]).astype(o_ref.dtype)


def rmsnorm(x, gain, layer, out_dtype, tm=256):
    m, d = x.shape
    gain3 = gain.reshape(-1, 1, d)
    return pl.pallas_call(
        _rmsnorm_kernel,
        grid=(m // tm,),
        in_specs=[pl.BlockSpec((tm, d), lambda i: (i, 0)),
                  pl.BlockSpec((None, 1, d), lambda i: (layer, 0, 0))],
        out_specs=pl.BlockSpec((tm, d), lambda i: (i, 0)),
        out_shape=jax.ShapeDtypeStruct((m, d), out_dtype),
        compiler_params=_params("arbitrary"),
        name="rmsnorm",
    )(x, gain3)


def _proj_kernel(x_ref, w_ref, *rest, has_res):
    if has_res:
        r_ref, o_ref, wb_ref = rest
    else:
        o_ref, wb_ref = rest

    @pl.when(pl.program_id(1) == 0)
    def _():
        wb_ref[...] = w_ref[...].astype(BF16)

    acc = jnp.dot(x_ref[...], wb_ref[...], preferred_element_type=F32)
    if has_res:
        acc = acc + r_ref[...]
    o_ref[...] = acc.astype(o_ref.dtype)


def proj(x, w, layer, out_dtype, residual=None, tm=1024, tn=512):
    m, k = x.shape
    n = w.shape[-1]
    in_specs = [pl.BlockSpec((tm, k), lambda j, i: (i, 0)),
                pl.BlockSpec((None, k, tn), lambda j, i: (layer, 0, j))]
    args = [x, w]
    if residual is not None:
        in_specs.append(pl.BlockSpec((tm, tn), lambda j, i: (i, j)))
        args.append(residual)
    return pl.pallas_call(
        functools.partial(_proj_kernel, has_res=residual is not None),
        grid=(n // tn, m // tm),
        in_specs=in_specs,
        out_specs=pl.BlockSpec((tm, tn), lambda j, i: (i, j)),
        out_shape=jax.ShapeDtypeStruct((m, n), out_dtype),
        scratch_shapes=[pltpu.VMEM((k, tn), BF16)],
        compiler_params=_params("arbitrary", "arbitrary"),
        name="proj",
    )(*args)


def _ffn_in_kernel(x_ref, wg_ref, wu_ref, cwg_ref, cwu_ref, bg_ref, bu_ref, o_ref,
                   wb_ref, hbuf_ref, *, tm, tn, ni, tiles_per_seq, chunk, mslab):
    s = pl.program_id(0)

    @pl.when(s == 0)
    def _():
        hbuf_ref[...] = jnp.zeros(hbuf_ref.shape, F32)

    @pl.when(s % ni == 0)
    def _():
        wb_ref[:, :tn] = wg_ref[...].astype(BF16)
        wb_ref[:, tn:] = wu_ref[...].astype(BF16)

    @pl.when((jnp.maximum(s - 1, 0) % ni) % tiles_per_seq == 0)
    def _():
        hbuf_ref[0:8, :] = jnp.zeros((8, 2 * tn), F32)

    cw = jnp.concatenate([cwg_ref[...], cwu_ref[...]], axis=1)
    bias = jnp.concatenate([bg_ref[...], bu_ref[...]], axis=1)
    w0, w1, w2 = cw[0:1], cw[1:2], cw[2:3]

    def conv_gate(r0, r1):
        for r in range(r0, r1, chunk):
            h0 = hbuf_ref[8 + r:8 + r + chunk, :]
            h1 = hbuf_ref[7 + r:7 + r + chunk, :]
            h2 = hbuf_ref[6 + r:6 + r + chunk, :]
            y = ((bias + w0 * h2) + w1 * h1) + w2 * h0
            gate, up = y[:, :tn], y[:, tn:]
            act = (gate * jax.nn.sigmoid(gate)) * up
            o_ref[r:r + chunk, :] = act.astype(o_ref.dtype)

    conv_gate(0, mslab)
    for c in range(tm // mslab):
        r = c * mslab
        if r + mslab < tm:
            conv_gate(r + mslab, r + 2 * mslab)
        else:
            hbuf_ref[0:8, :] = hbuf_ref[tm:tm + 8, :]
        hbuf_ref[8 + r:8 + r + mslab, :] = jnp.dot(x_ref[r:r + mslab, :], wb_ref[...],
                                                   preferred_element_type=F32)


def ffn_in(x, w_in, conv_w, conv_b, layer, seq, tm=1024, tn=256, chunk=64, mslab=128):
    m, k = x.shape
    d_ff = w_in.shape[-1] // 2
    nj = d_ff // tn
    ni = m // tm
    last = nj * ni - 1
    conv_b3 = conv_b.reshape(conv_b.shape[0], 1, 2 * d_ff)

    def mm_col(s):
        return jnp.minimum(s, last) // ni

    def ep_col(s):
        return jnp.maximum(s - 1, 0) // ni

    return pl.pallas_call(
        functools.partial(_ffn_in_kernel, tm=tm, tn=tn, ni=ni, tiles_per_seq=seq // tm,
                          chunk=chunk, mslab=mslab),
        grid=(nj * ni + 1,),
        in_specs=[pl.BlockSpec((tm, k), lambda s: (jnp.minimum(s, last) % ni, 0)),
                  pl.BlockSpec((None, k, tn), lambda s: (layer, 0, mm_col(s))),
                  pl.BlockSpec((None, k, tn), lambda s: (layer, 0, mm_col(s) + nj)),
                  pl.BlockSpec((None, CONV_WIDTH, tn), lambda s: (layer, 0, ep_col(s))),
                  pl.BlockSpec((None, CONV_WIDTH, tn), lambda s: (layer, 0, ep_col(s) + nj)),
                  pl.BlockSpec((None, 1, tn), lambda s: (layer, 0, ep_col(s))),
                  pl.BlockSpec((None, 1, tn), lambda s: (layer, 0, ep_col(s) + nj))],
        out_specs=pl.BlockSpec((tm, tn), lambda s: (jnp.maximum(s - 1, 0) % ni, ep_col(s))),
        out_shape=jax.ShapeDtypeStruct((m, d_ff), BF16),
        scratch_shapes=[pltpu.VMEM((k, 2 * tn), BF16),
                        pltpu.VMEM((tm + 8, 2 * tn), F32)],
        compiler_params=_params("arbitrary"),
        name="ffn_in",
    )(x, w_in, w_in, conv_w, conv_w, conv_b3, conv_b3)


def _ffn_down_kernel(x_ref, w_ref, r_ref, o_ref):
    @pl.when(pl.program_id(2) == 0)
    def _():
        o_ref[...] = r_ref[...]

    o_ref[...] += jnp.dot(x_ref[...], w_ref[...].astype(BF16), preferred_element_type=F32)


def ffn_down(x, w, layer, residual, tm=2048, tn=1024, tk=1024):
    m, k = x.shape
    n = w.shape[-1]
    return pl.pallas_call(
        _ffn_down_kernel,
        grid=(m // tm, n // tn, k // tk),
        in_specs=[pl.BlockSpec((tm, tk), lambda i, j, kk: (i, kk)),
                  pl.BlockSpec((None, tk, tn), lambda i, j, kk: (layer, kk, j)),
                  pl.BlockSpec((tm, tn), lambda i, j, kk: (i, j))],
        out_specs=pl.BlockSpec((tm, tn), lambda i, j, kk: (i, j)),
        out_shape=jax.ShapeDtypeStruct((m, n), F32),
        compiler_params=_params("arbitrary", "arbitrary", "arbitrary"),
        name="ffn_down",
    )(x, w, residual)


def _sb_kernel(q_ref, k_ref, v_ref, tri_ref, o_ref, *, seq, tile, group, scale):
    t = tile
    d = SB_HEAD_DIM
    tri2 = tri_ref[...]
    row = lax.broadcasted_iota(jnp.int32, (t, t), 0)
    col = lax.broadcasted_iota(jnp.int32, (t, t), 1)
    strict = col < row
    sign = jnp.uint32(0x80000000)

    heads = list(range(group))

    def key_tile(ref, g, kb):
        return ref[pl.ds(pl.multiple_of(kb * t, t), t), g * d:(g + 1) * d]

    def logits(qs, gs, kbs):
        return [lax.dot_general(q, key_tile(k_ref, g, kb), (((1,), (1,)), ((), ())),
                                preferred_element_type=F32)
                for q, g, kb in zip(qs, gs, kbs)]

    def suffix_sums(z_raw, masked):
        out = []
        zs = [z * scale for z in z_raw]
        sps = []
        for z in zs:
            neg_abs = lax.bitcast_convert_type(lax.bitcast_convert_type(z, jnp.uint32) | sign, F32)
            sp = jnp.maximum(z, 0.0) + jnp.log(1.0 + jnp.exp(neg_abs))
            sps.append(jnp.where(strict, sp, 0.0) if masked else sp)
        for z, sp in zip(zs, sps):
            hi = sp.astype(BF16)
            lo = (sp - hi.astype(F32)).astype(BF16)
            cum = jnp.dot(jnp.concatenate([hi, lo], axis=1), tri2,
                          preferred_element_type=F32)
            rowsum = cum[:, 0:1] + sp[:, 0:1]
            if masked:
                cum = jnp.where(strict, cum, -NEG_BIG)
            out.append((z - sp, cum, rowsum))
        return out

    def accumulate(parts, gs, kbs, cs, accs):
        ws = [jnp.exp((p[0] - p[1]) - c).astype(BF16) for p, c in zip(parts, cs)]
        accs = [a + jnp.dot(w, key_tile(v_ref, g, kb), preferred_element_type=F32)
                for a, w, g, kb in zip(accs, ws, gs, kbs)]
        cs = [c + p[2] for c, p in zip(cs, parts)]
        return cs, accs

    def visit(qs, gs, kbs, masked, cs, accs):
        return accumulate(suffix_sums(logits(qs, gs, kbs), masked), gs, kbs, cs, accs)

    def q_body(qp, carry):
        q0 = 2 * qp
        q1 = q0 + 1
        qs = pl.multiple_of(qp * 2 * t, 2 * t)
        qboth = [q_ref[pl.ds(qs, 2 * t), g * d:(g + 1) * d] for g in heads]
        qa = [q[:t] for q in qboth]
        qb = [q[t:] for q in qboth]
        zc = [jnp.zeros((t, 1), F32)] * (2 * group)
        za = [jnp.zeros((t, d), F32)] * (2 * group)
        cs, accs = visit(qa + qb, heads + heads, [q0] * group + [q1] * group, True, zc, za)
        cb, ab = visit(qb, heads, [q0] * group, False, cs[group:], accs[group:])
        cs = [jnp.concatenate([cs[g], cb[g]], axis=0) for g in heads]
        accs = [jnp.concatenate([accs[g], ab[g]], axis=0) for g in heads]
        zs = logits(qboth, heads, [jnp.maximum(q0 - 1, 0)] * group)

        def k_body(n, st):
            kb = q0 - 1 - n
            z_next = logits(qboth, heads, [jnp.maximum(kb - 1, 0)] * group)
            c_new, a_new = accumulate(suffix_sums(list(st[0]), False), heads, [kb] * group,
                                      list(st[1]), list(st[2]))
            return tuple(z_next), tuple(c_new), tuple(a_new)

        _, cs, accs = lax.fori_loop(0, q0, k_body, (tuple(zs), tuple(cs), tuple(accs)))
        for g in heads:
            o_ref[pl.ds(qs, 2 * t), g * d:(g + 1) * d] = accs[g].astype(o_ref.dtype)
        return carry

    lax.fori_loop(0, seq // (2 * t), q_body, 0)


def sb_attention(qkv, batch, seq, tile=256, group=4):
    m = qkv.shape[0]
    heads = qkv.shape[1] // (3 * SB_HEAD_DIM)
    ng = heads // group
    gw = group * SB_HEAD_DIM
    idx = jnp.arange(tile)
    tri = (idx[:, None] > idx[None, :]).astype(BF16)
    tri2 = jnp.concatenate([tri, tri], axis=0)
    return pl.pallas_call(
        functools.partial(_sb_kernel, seq=seq, tile=tile, group=group, scale=SB_HEAD_DIM ** -0.5),
        grid=(batch, ng),
        in_specs=[pl.BlockSpec((seq, gw), lambda b, h: (b, h)),
                  pl.BlockSpec((seq, gw), lambda b, h: (b, ng + h)),
                  pl.BlockSpec((seq, gw), lambda b, h: (b, 2 * ng + h)),
                  pl.BlockSpec((2 * tile, tile), lambda b, h: (0, 0))],
        out_specs=pl.BlockSpec((seq, gw), lambda b, h: (b, h)),
        out_shape=jax.ShapeDtypeStruct((m, heads * SB_HEAD_DIM), BF16),
        compiler_params=_params("arbitrary", "arbitrary"),
        name="sb_attention",
    )(qkv, qkv, qkv, tri2)


def _swa_kernel(slopes_ref, sinks_ref, q_ref, k_ref, v_ref, o_ref, *, seq, scale):
    blk_sz = SWA_BLOCK
    kvh = pl.program_id(1)
    odd = (kvh % 2) == 1
    lane = lax.broadcasted_iota(jnp.int32, (2 * blk_sz, 128), 1)
    low = lane < SWA_HEAD_DIM

    q_idx = lax.broadcasted_iota(jnp.int32, (blk_sz, 2 * blk_sz), 0)
    k_idx = lax.broadcasted_iota(jnp.int32, (blk_sz, 2 * blk_sz), 1)
    dist = q_idx + blk_sz - k_idx
    in_window = (dist >= 0) & (dist < blk_sz)
    dist_f = dist.astype(F32)

    def place(band):
        x = band.astype(F32)
        x = jnp.where(odd, pltpu.roll(x, SWA_HEAD_DIM, 1), x)
        at0 = jnp.where(low, x, 0.0)
        at64 = pltpu.roll(at0, SWA_HEAD_DIM, 1)
        return at0.astype(BF16), at64.astype(BF16)

    def body(blk, carry):
        cur = pl.multiple_of(blk * blk_sz, blk_sz)
        prev = pl.multiple_of(jnp.maximum(blk - 1, 0) * blk_sz, blk_sz)
        kband = jnp.concatenate([k_ref[pl.ds(prev, blk_sz), :], k_ref[pl.ds(cur, blk_sz), :]], axis=0)
        vband = jnp.concatenate([v_ref[pl.ds(prev, blk_sz), :], v_ref[pl.ds(cur, blk_sz), :]], axis=0)
        k_at = place(kband)
        v_at = place(vband)
        valid = in_window & ((blk * blk_sz + k_idx - blk_sz) >= 0)
        for c in range(SWA_GROUP // 2):
            qc = q_ref[pl.ds(cur, blk_sz), c * 128:(c + 1) * 128]
            out = None
            for half in range(2):
                head = kvh * SWA_GROUP + 2 * c + half
                s = lax.dot_general(qc, k_at[half], (((1,), (1,)), ((), ())),
                                    preferred_element_type=F32)
                s = s * scale - slopes_ref[head] * dist_f
                s = jnp.where(valid, s, NEG_BIG)
                sink = sinks_ref[head]
                mx = jnp.maximum(jnp.max(s, axis=-1, keepdims=True), sink)
                p = jnp.exp(s - mx)
                denom = jnp.sum(p, axis=-1, keepdims=True) + jnp.exp(sink - mx)
                o = jnp.dot(p.astype(BF16), v_at[half], preferred_element_type=F32) / denom
                out = o if out is None else out + o
            o_ref[pl.ds(cur, blk_sz), c * 128:(c + 1) * 128] = out.astype(o_ref.dtype)
        return carry

    lax.fori_loop(0, seq // blk_sz, body, 0)


def swa_attention(qkv, slopes, sinks, batch, seq, heads):
    m = qkv.shape[0]
    kv_heads = heads // SWA_GROUP
    qd = heads * SWA_HEAD_DIM
    gw = SWA_GROUP * SWA_HEAD_DIM
    k_blk0 = qd // 128
    v_blk0 = (qd + kv_heads * SWA_HEAD_DIM) // 128
    smem = pl.BlockSpec(memory_space=pltpu.SMEM)
    return pl.pallas_call(
        functools.partial(_swa_kernel, seq=seq, scale=SWA_HEAD_DIM ** -0.5),
        grid=(batch, kv_heads),
        in_specs=[smem, smem,
                  pl.BlockSpec((seq, gw), lambda b, h: (b, h)),
                  pl.BlockSpec((seq, 128), lambda b, h: (b, k_blk0 + h // 2)),
                  pl.BlockSpec((seq, 128), lambda b, h: (b, v_blk0 + h // 2))],
        out_specs=pl.BlockSpec((seq, gw), lambda b, h: (b, h)),
        out_shape=jax.ShapeDtypeStruct((m, qd), BF16),
        compiler_params=_params("arbitrary", "arbitrary"),
        name="swa_attention",
    )(slopes, sinks, qkv, qkv, qkv)


def kernel(x, attn_norm, ffn_norm, sb_w_qkv, sb_w_o, swa_w_qkv, swa_w_o, swa_sinks,
           ffn_w_in, ffn_conv_w, ffn_conv_b, ffn_w_down, final_norm):
    batch, seq, d = x.shape
    depth = attn_norm.shape[0]
    swa_heads = swa_sinks.shape[-1]
    slopes = jnp.asarray(2.0 ** (-8.0 * np.arange(1, swa_heads + 1) / swa_heads), F32)
    h = x.reshape(batch * seq, d)
    for i in range(depth):
        xn = rmsnorm(h, attn_norm, i, BF16)
        j = i // 2
        if i % 2 == 0:
            qkv = proj(xn, sb_w_qkv, j, BF16)
            mix = sb_attention(qkv, batch, seq)
            h = proj(mix, sb_w_o, j, F32, residual=h)
        else:
            qkv = proj(xn, swa_w_qkv, j, BF16)
            mix = swa_attention(qkv, slopes, swa_sinks[j], batch, seq, swa_heads)
            h = proj(mix, swa_w_o, j, F32, residual=h)
        xn = rmsnorm(h, ffn_norm, i, BF16)
        act = ffn_in(xn, ffn_w_in, ffn_conv_w, ffn_conv_b, i, seq)
        h = ffn_down(act, ffn_w_down, i, h)
    out = rmsnorm(h, final_norm.reshape(1, d), 0, F32)
    return out.reshape(batch, seq, d)
```

```python
import functools

import jax
import jax.numpy as jnp
import numpy as np
from jax import lax
from jax.experimental import pallas as pl
from jax.experimental.pallas import tpu as pltpu

F32 = jnp.float32
BF16 = jnp.bfloat16

RMS_EPS = 1e-6
SB_HEAD_DIM = 128
SWA_HEAD_DIM = 64
SWA_GROUP = 8
SWA_BLOCK = 128
CONV_WIDTH = 3
NEG_BIG = -1e30
LOG2E = 1.4426950408889634

VMEM_LIMIT = 56 * 1024 * 1024


def _params(*sem):
    return pltpu.CompilerParams(dimension_semantics=sem, vmem_limit_bytes=VMEM_LIMIT)


def _rmsnorm_kernel(x_ref, g_ref, o_ref):
    x = x_ref[...]
    ms = jnp.mean(x * x, axis=-1, keepdims=True)
    o_ref[...] = ((x * lax.rsqrt(ms + RMS_EPS)) * g_ref[...]).astype(o_ref.dtype)


def rmsnorm(x, gain, layer, out_dtype, tm=256):
    m, d = x.shape
    gain3 = gain.reshape(-1, 1, d)
    return pl.pallas_call(
        _rmsnorm_kernel,
        grid=(m // tm,),
        in_specs=[pl.BlockSpec((tm, d), lambda i: (i, 0)),
                  pl.BlockSpec((None, 1, d), lambda i: (layer, 0, 0))],
        out_specs=pl.BlockSpec((tm, d), lambda i: (i, 0)),
        out_shape=jax.ShapeDtypeStruct((m, d), out_dtype),
        compiler_params=_params("arbitrary"),
        name="rmsnorm",
    )(x, gain3)


def _proj_kernel(x_ref, w_ref, *rest, has_res):
    if has_res:
        r_ref, o_ref, wb_ref = rest
    else:
        o_ref, wb_ref = rest

    @pl.when(pl.program_id(1) == 0)
    def _():
        wb_ref[...] = w_ref[...].astype(BF16)

    acc = jnp.dot(x_ref[...], wb_ref[...], preferred_element_type=F32)
    if has_res:
        acc = acc + r_ref[...]
    o_ref[...] = acc.astype(o_ref.dtype)


def proj(x, w, layer, out_dtype, residual=None, tm=1024, tn=512):
    m, k = x.shape
    n = w.shape[-1]
    in_specs = [pl.BlockSpec((tm, k), lambda j, i: (i, 0)),
                pl.BlockSpec((None, k, tn), lambda j, i: (layer, 0, j))]
    args = [x, w]
    if residual is not None:
        in_specs.append(pl.BlockSpec((tm, tn), lambda j, i: (i, j)))
        args.append(residual)
    return pl.pallas_call(
        functools.partial(_proj_kernel, has_res=residual is not None),
        grid=(n // tn, m // tm),
        in_specs=in_specs,
        out_specs=pl.BlockSpec((tm, tn), lambda j, i: (i, j)),
        out_shape=jax.ShapeDtypeStruct((m, n), out_dtype),
        scratch_shapes=[pltpu.VMEM((k, tn), BF16)],
        compiler_params=_params("arbitrary", "arbitrary"),
        name="proj",
    )(*args)


def _ffn_in_kernel(x_ref, wg_ref, wu_ref, cwg_ref, cwu_ref, bg_ref, bu_ref, o_ref,
                   wb_ref, hbuf_ref, *, tm, tn, ni, tiles_per_seq, chunk, mslab):
    s = pl.program_id(0)

    @pl.when(s == 0)
    def _():
        hbuf_ref[...] = jnp.zeros(hbuf_ref.shape, F32)

    @pl.when(s % ni == 0)
    def _():
        wb_ref[:, :tn] = wg_ref[...].astype(BF16)
        wb_ref[:, tn:] = wu_ref[...].astype(BF16)

    @pl.when((jnp.maximum(s - 1, 0) % ni) % tiles_per_seq == 0)
    def _():
        hbuf_ref[0:8, :] = jnp.zeros((8, 2 * tn), F32)

    cw = jnp.concatenate([cwg_ref[...], cwu_ref[...]], axis=1)
    bias = jnp.concatenate([bg_ref[...], bu_ref[...]], axis=1)
    w0, w1, w2 = cw[0:1], cw[1:2], cw[2:3]

    def conv_gate(r0, r1):
        for r in range(r0, r1, chunk):
            h0 = hbuf_ref[8 + r:8 + r + chunk, :]
            h1 = hbuf_ref[7 + r:7 + r + chunk, :]
            h2 = hbuf_ref[6 + r:6 + r + chunk, :]
            y = ((bias + w0 * h2) + w1 * h1) + w2 * h0
            gate, up = y[:, :tn], y[:, tn:]
            act = (gate * jax.nn.sigmoid(gate)) * up
            o_ref[r:r + chunk, :] = act.astype(o_ref.dtype)

    conv_gate(0, mslab)
    for c in range(tm // mslab):
        r = c * mslab
        if r + mslab < tm:
            conv_gate(r + mslab, r + 2 * mslab)
        else:
            hbuf_ref[0:8, :] = hbuf_ref[tm:tm + 8, :]
        hbuf_ref[8 + r:8 + r + mslab, :] = jnp.dot(x_ref[r:r + mslab, :], wb_ref[...],
                                                   preferred_element_type=F32)


def ffn_in(x, w_in, conv_w, conv_b, layer, seq, tm=1024, tn=256, chunk=64, mslab=128):
    m, k = x.shape
    d_ff = w_in.shape[-1] // 2
    nj = d_ff // tn
    ni = m // tm
    last = nj * ni - 1
    conv_b3 = conv_b.reshape(conv_b.shape[0], 1, 2 * d_ff)

    def mm_col(s):
        return jnp.minimum(s, last) // ni

    def ep_col(s):
        return jnp.maximum(s - 1, 0) // ni

    return pl.pallas_call(
        functools.partial(_ffn_in_kernel, tm=tm, tn=tn, ni=ni, tiles_per_seq=seq // tm,
                          chunk=chunk, mslab=mslab),
        grid=(nj * ni + 1,),
        in_specs=[pl.BlockSpec((tm, k), lambda s: (jnp.minimum(s, last) % ni, 0)),
                  pl.BlockSpec((None, k, tn), lambda s: (layer, 0, mm_col(s))),
                  pl.BlockSpec((None, k, tn), lambda s: (layer, 0, mm_col(s) + nj)),
                  pl.BlockSpec((None, CONV_WIDTH, tn), lambda s: (layer, 0, ep_col(s))),
                  pl.BlockSpec((None, CONV_WIDTH, tn), lambda s: (layer, 0, ep_col(s) + nj)),
                  pl.BlockSpec((None, 1, tn), lambda s: (layer, 0, ep_col(s))),
                  pl.BlockSpec((None, 1, tn), lambda s: (layer, 0, ep_col(s) + nj))],
        out_specs=pl.BlockSpec((tm, tn), lambda s: (jnp.maximum(s - 1, 0) % ni, ep_col(s))),
        out_shape=jax.ShapeDtypeStruct((m, d_ff), BF16),
        scratch_shapes=[pltpu.VMEM((k, 2 * tn), BF16),
                        pltpu.VMEM((tm + 8, 2 * tn), F32)],
        compiler_params=_params("arbitrary"),
        name="ffn_in",
    )(x, w_in, w_in, conv_w, conv_w, conv_b3, conv_b3)


def _ffn_down_kernel(x_ref, w_ref, r_ref, o_ref):
    @pl.when(pl.program_id(2) == 0)
    def _():
        o_ref[...] = r_ref[...]

    o_ref[...] += jnp.dot(x_ref[...], w_ref[...].astype(BF16), preferred_element_type=F32)


def ffn_down(x, w, layer, residual, tm=2048, tn=1024, tk=1024):
    m, k = x.shape
    n = w.shape[-1]
    return pl.pallas_call(
        _ffn_down_kernel,
        grid=(m // tm, n // tn, k // tk),
        in_specs=[pl.BlockSpec((tm, tk), lambda i, j, kk: (i, kk)),
                  pl.BlockSpec((None, tk, tn), lambda i, j, kk: (layer, kk, j)),
                  pl.BlockSpec((tm, tn), lambda i, j, kk: (i, j))],
        out_specs=pl.BlockSpec((tm, tn), lambda i, j, kk: (i, j)),
        out_shape=jax.ShapeDtypeStruct((m, n), F32),
        compiler_params=_params("arbitrary", "arbitrary", "arbitrary"),
        name="ffn_down",
    )(x, w, residual)


def _sb_kernel(q_ref, k_ref, v_ref, tri_ref, o_ref, c0_ref, acc0_ref, z_ref, hi_ref, zs_ref,
               cum_ref, w_ref, sp0_ref, r_ref, c_ref, acc_ref, *, seq, tile, group, scale):
    t = tile
    t2 = 2 * t
    npairs = seq // t2
    d = SB_HEAD_DIM
    tri = tri_ref[...]
    row = lax.broadcasted_iota(jnp.int32, (t, t), 0)
    col = lax.broadcasted_iota(jnp.int32, (t, t), 1)
    strict = col < row
    sign = jnp.uint32(0x80000000)

    heads = list(range(group))

    def key_tile(ref, g, kb):
        return ref[pl.ds(pl.multiple_of(kb * t, t), t), g * d:(g + 1) * d]

    def logits(qs, gs, kbs):
        return [lax.dot_general(q, key_tile(k_ref, g, kb), (((1,), (1,)), ((), ())),
                                preferred_element_type=F32)
                for q, g, kb in zip(qs, gs, kbs)]

    def suffix_sums(z_raw, masked):
        out = []
        zs = [z * (scale * LOG2E) for z in z_raw]
        sps = []
        for z in zs:
            neg_abs = lax.bitcast_convert_type(lax.bitcast_convert_type(z, jnp.uint32) | sign, F32)
            sp = jnp.maximum(z, 0.0) + jnp.log2(1.0 + jnp.exp2(neg_abs))
            sps.append(jnp.where(strict, sp, 0.0) if masked else sp)
        for z, sp in zip(zs, sps):
            cum = jnp.dot(sp.astype(BF16), tri, preferred_element_type=F32)
            rowsum = cum[:, 0:1] + sp[:, 0:1]
            if masked:
                cum = jnp.where(strict, cum, -NEG_BIG)
            out.append((z - sp, cum, rowsum))
        return out

    def accumulate(parts, gs, kbs, cs, accs):
        ws = [jnp.exp2((p[0] - p[1]) - c).astype(BF16) for p, c in zip(parts, cs)]
        accs = [a + jnp.dot(w, key_tile(v_ref, g, kb), preferred_element_type=F32)
                for a, w, g, kb in zip(accs, ws, gs, kbs)]
        cs = [c + p[2] for c, p in zip(cs, parts)]
        return cs, accs

    def visit(qs, gs, kbs, masked, cs, accs):
        return accumulate(suffix_sums(logits(qs, gs, kbs), masked), gs, kbs, cs, accs)

    def diag_body(qp, carry):
        q0 = 2 * qp
        q1 = q0 + 1
        rows = pl.ds(pl.multiple_of(qp * t2, t2), t2)
        qboth = [q_ref[rows, g * d:(g + 1) * d] for g in heads]
        qa = [q[:t] for q in qboth]
        qb = [q[t:] for q in qboth]
        zc = [jnp.zeros((t, 1), F32)] * (2 * group)
        za = [jnp.zeros((t, d), F32)] * (2 * group)
        cs, accs = visit(qa + qb, heads + heads, [q0] * group + [q1] * group, True, zc, za)
        cb, ab = visit(qb, heads, [q0] * group, False, cs[group:], accs[group:])
        for g in heads:
            acc = jnp.concatenate([accs[g], ab[g]], axis=0)
            c0_ref[g, rows, :] = jnp.concatenate([cs[g], cb[g]], axis=0)
            acc0_ref[rows, g * d:(g + 1) * d] = acc
            o_ref[rows, g * d:(g + 1) * d] = acc.astype(o_ref.dtype)
        return carry

    lax.fori_loop(0, npairs, diag_body, 0)

    for buf in (z_ref, hi_ref, zs_ref, cum_ref, w_ref, sp0_ref, r_ref, c_ref, acc_ref):
        buf[...] = jnp.zeros(buf.shape, buf.dtype)

    def step(i, hist):
        (q1, k1), (q2, k2), (q3, k3), (q4, k4), (q5, k5) = hist
        slot = i % 2

        rows5 = pl.ds(pl.multiple_of(q5 * t2, t2), t2)
        first5 = k5 == 2 * q5 - 1
        for g in heads:
            a_in = jnp.where(first5, acc0_ref[rows5, g * d:(g + 1) * d], acc_ref[g])
            a = a_in + jnp.dot(w_ref[g], key_tile(v_ref, g, k5), preferred_element_type=F32)
            acc_ref[g] = a
            o_ref[rows5, g * d:(g + 1) * d] = a.astype(o_ref.dtype)

        rows4 = pl.ds(pl.multiple_of(q4 * t2, t2), t2)
        first4 = k4 == 2 * q4 - 1
        for g in heads:
            c_in = jnp.where(first4, c0_ref[g, rows4, :], c_ref[g])
            w_ref[g] = jnp.exp2((zs_ref[slot, g] - cum_ref[g]) - c_in).astype(BF16)
            c_ref[g] = c_in + r_ref[g]

        for g in heads:
            cum = jnp.dot(hi_ref[g], tri, preferred_element_type=F32)
            cum_ref[g] = cum
            r_ref[g] = cum[:, 0:1] + sp0_ref[g]

        for g in heads:
            z = z_ref[g] * (scale * LOG2E)
            neg_abs = lax.bitcast_convert_type(lax.bitcast_convert_type(z, jnp.uint32) | sign, F32)
            sp = jnp.maximum(z, 0.0) + jnp.log2(1.0 + jnp.exp2(neg_abs))
            hi_ref[g] = sp.astype(BF16)
            zs_ref[slot, g] = z - sp
            sp0_ref[g] = sp[:, 0:1]

        rows1 = pl.ds(pl.multiple_of(q1 * t2, t2), t2)
        for g in heads:
            z_ref[g] = lax.dot_general(q_ref[rows1, g * d:(g + 1) * d], key_tile(k_ref, g, k1),
                                       (((1,), (1,)), ((), ())), preferred_element_type=F32)

        wrap = k1 == 0
        qn = jnp.where(wrap, q1 + 1, q1)
        kn = jnp.where(wrap, 2 * qn - 1, k1 - 1)
        done = qn >= npairs
        qn = jnp.where(done, q1, qn)
        kn = jnp.where(done, k1, kn)
        return ((qn, kn), (q1, k1), (q2, k2), (q3, k3), (q4, k4))

    if npairs > 1:
        one = jnp.int32(1)
        lax.fori_loop(0, npairs * (npairs - 1) + 4, step, ((one, one),) * 5, unroll=2)


def sb_attention(qkv, batch, seq, tile=256, group=2):
    m = qkv.shape[0]
    heads = qkv.shape[1] // (3 * SB_HEAD_DIM)
    ng = heads // group
    d = SB_HEAD_DIM
    gw = group * d
    t2 = 2 * tile
    idx = jnp.arange(tile)
    tri = (idx[:, None] > idx[None, :]).astype(BF16)
    return pl.pallas_call(
        functools.partial(_sb_kernel, seq=seq, tile=tile, group=group, scale=d ** -0.5),
        grid=(batch, ng),
        in_specs=[pl.BlockSpec((seq, gw), lambda b, h: (b, h)),
                  pl.BlockSpec((seq, gw), lambda b, h: (b, ng + h)),
                  pl.BlockSpec((seq, gw), lambda b, h: (b, 2 * ng + h)),
                  pl.BlockSpec((tile, tile), lambda b, h: (0, 0))],
        out_specs=pl.BlockSpec((seq, gw), lambda b, h: (b, h)),
        out_shape=jax.ShapeDtypeStruct((m, heads * d), BF16),
        scratch_shapes=[pltpu.VMEM((group, seq, 1), F32),
                        pltpu.VMEM((seq, gw), F32),
                        pltpu.VMEM((group, t2, tile), F32),
                        pltpu.VMEM((group, t2, tile), BF16),
                        pltpu.VMEM((2, group, t2, tile), F32),
                        pltpu.VMEM((group, t2, tile), F32),
                        pltpu.VMEM((group, t2, tile), BF16),
                        pltpu.VMEM((group, t2, 1), F32),
                        pltpu.VMEM((group, t2, 1), F32),
                        pltpu.VMEM((group, t2, 1), F32),
                        pltpu.VMEM((group, t2, d), F32)],
        compiler_params=_params("arbitrary", "arbitrary"),
        name="sb_attention",
    )(qkv, qkv, qkv, tri)


def _swa_kernel(slopes_ref, sinks_ref, q_ref, k_ref, v_ref, o_ref, *, seq, scale):
    blk_sz = SWA_BLOCK
    kvh = pl.program_id(1)
    odd = (kvh % 2) == 1
    lane = lax.broadcasted_iota(jnp.int32, (2 * blk_sz, 128), 1)
    low = lane < SWA_HEAD_DIM

    q_idx = lax.broadcasted_iota(jnp.int32, (blk_sz, 2 * blk_sz), 0)
    k_idx = lax.broadcasted_iota(jnp.int32, (blk_sz, 2 * blk_sz), 1)
    dist = q_idx + blk_sz - k_idx
    in_window = (dist >= 0) & (dist < blk_sz)
    dist_f = dist.astype(F32)

    def place(band):
        x = band.astype(F32)
        x = jnp.where(odd, pltpu.roll(x, SWA_HEAD_DIM, 1), x)
        at0 = jnp.where(low, x, 0.0)
        at64 = pltpu.roll(at0, SWA_HEAD_DIM, 1)
        return at0.astype(BF16), at64.astype(BF16)

    def body(blk, carry):
        cur = pl.multiple_of(blk * blk_sz, blk_sz)
        prev = pl.multiple_of(jnp.maximum(blk - 1, 0) * blk_sz, blk_sz)
        kband = jnp.concatenate([k_ref[pl.ds(prev, blk_sz), :], k_ref[pl.ds(cur, blk_sz), :]], axis=0)
        vband = jnp.concatenate([v_ref[pl.ds(prev, blk_sz), :], v_ref[pl.ds(cur, blk_sz), :]], axis=0)
        k_at = place(kband)
        v_at = place(vband)
        valid = in_window & ((blk * blk_sz + k_idx - blk_sz) >= 0)
        for c in range(SWA_GROUP // 2):
            qc = q_ref[pl.ds(cur, blk_sz), c * 128:(c + 1) * 128]
            out = None
            for half in range(2):
                head = kvh * SWA_GROUP + 2 * c + half
                s = lax.dot_general(qc, k_at[half], (((1,), (1,)), ((), ())),
                                    preferred_element_type=F32)
                s = s * scale - slopes_ref[head] * dist_f
                s = jnp.where(valid, s, NEG_BIG)
                sink = sinks_ref[head]
                mx = jnp.maximum(jnp.max(s, axis=-1, keepdims=True), sink)
                p = jnp.exp(s - mx)
                denom = jnp.sum(p, axis=-1, keepdims=True) + jnp.exp(sink - mx)
                o = jnp.dot(p.astype(BF16), v_at[half], preferred_element_type=F32) / denom
                out = o if out is None else out + o
            o_ref[pl.ds(cur, blk_sz), c * 128:(c + 1) * 128] = out.astype(o_ref.dtype)
        return carry

    lax.fori_loop(0, seq // blk_sz, body, 0)


def swa_attention(qkv, slopes, sinks, batch, seq, heads):
    m = qkv.shape[0]
    kv_heads = heads // SWA_GROUP
    qd = heads * SWA_HEAD_DIM
    gw = SWA_GROUP * SWA_HEAD_DIM
    k_blk0 = qd // 128
    v_blk0 = (qd + kv_heads * SWA_HEAD_DIM) // 128
    smem = pl.BlockSpec(memory_space=pltpu.SMEM)
    return pl.pallas_call(
        functools.partial(_swa_kernel, seq=seq, scale=SWA_HEAD_DIM ** -0.5),
        grid=(batch, kv_heads),
        in_specs=[smem, smem,
                  pl.BlockSpec((seq, gw), lambda b, h: (b, h)),
                  pl.BlockSpec((seq, 128), lambda b, h: (b, k_blk0 + h // 2)),
                  pl.BlockSpec((seq, 128), lambda b, h: (b, v_blk0 + h // 2))],
        out_specs=pl.BlockSpec((seq, gw), lambda b, h: (b, h)),
        out_shape=jax.ShapeDtypeStruct((m, qd), BF16),
        compiler_params=_params("arbitrary", "arbitrary"),
        name="swa_attention",
    )(slopes, sinks, qkv, qkv, qkv)


def kernel(x, attn_norm, ffn_norm, sb_w_qkv, sb_w_o, swa_w_qkv, swa_w_o, swa_sinks,
           ffn_w_in, ffn_conv_w, ffn_conv_b, ffn_w_down, final_norm):
    batch, seq, d = x.shape
    depth = attn_norm.shape[0]
    swa_heads = swa_sinks.shape[-1]
    slopes = jnp.asarray(2.0 ** (-8.0 * np.arange(1, swa_heads + 1) / swa_heads), F32)
    h = x.reshape(batch * seq, d)
    for i in range(depth):
        xn = rmsnorm(h, attn_norm, i, BF16)
        j = i // 2
        if i % 2 == 0:
            qkv = proj(xn, sb_w_qkv, j, BF16)
            mix = sb_attention(qkv, batch, seq)
            h = proj(mix, sb_w_o, j, F32, residual=h)
        else:
            qkv = proj(xn, swa_w_qkv, j, BF16)
            mix = swa_attention(qkv, slopes, swa_sinks[j], batch, seq, swa_heads)
            h = proj(mix, swa_w_o, j, F32, residual=h)
        xn = rmsnorm(h, ffn_norm, i, BF16)
        act = ffn_in(xn, ffn_w_in, ffn_conv_w, ffn_conv_b, i, seq)
        h = ffn_down(act, ffn_w_down, i, h)
    out = rmsnorm(h, final_norm.reshape(1, d), 0, F32)
    return out.reshape(batch, seq, d)
```

```python
import functools

import jax
import jax.numpy as jnp
import numpy as np
from jax import lax
from jax.experimental import pallas as pl
from jax.experimental.pallas import tpu as pltpu

F32 = jnp.float32
BF16 = jnp.bfloat16

RMS_EPS = 1e-6
SB_HEAD_DIM = 128
SWA_HEAD_DIM = 64
SWA_GROUP = 8
SWA_BLOCK = 128
CONV_WIDTH = 3
NEG_BIG = -1e30
LOG2E = 1.4426950408889634

VMEM_LIMIT = 56 * 1024 * 1024


def _params(*sem):
    return pltpu.CompilerParams(dimension_semantics=sem, vmem_limit_bytes=VMEM_LIMIT)


def _rmsnorm_kernel(x_ref, g_ref, o_ref):
    x = x_ref[...]
    ms = jnp.mean(x * x, axis=-1, keepdims=True)
    o_ref[...] = ((x * lax.rsqrt(ms + RMS_EPS)) * g_ref[...]).astype(o_ref.dtype)


def rmsnorm(x, gain, layer, out_dtype, tm=256):
    m, d = x.shape
    gain3 = gain.reshape(-1, 1, d)
    return pl.pallas_call(
        _rmsnorm_kernel,
        grid=(m // tm,),
        in_specs=[pl.BlockSpec((tm, d), lambda i: (i, 0)),
                  pl.BlockSpec((None, 1, d), lambda i: (layer, 0, 0))],
        out_specs=pl.BlockSpec((tm, d), lambda i: (i, 0)),
        out_shape=jax.ShapeDtypeStruct((m, d), out_dtype),
        compiler_params=_params("arbitrary"),
        name="rmsnorm",
    )(x, gain3)


def _proj_kernel(x_ref, w_ref, *rest, has_res):
    if has_res:
        r_ref, o_ref, wb_ref = rest
    else:
        o_ref, wb_ref = rest

    @pl.when(pl.program_id(1) == 0)
    def _():
        wb_ref[...] = w_ref[...].astype(BF16)

    acc = jnp.dot(x_ref[...], wb_ref[...], preferred_element_type=F32)
    if has_res:
        acc = acc + r_ref[...]
    o_ref[...] = acc.astype(o_ref.dtype)


def proj(x, w, layer, out_dtype, residual=None, tm=1024, tn=512):
    m, k = x.shape
    n = w.shape[-1]
    in_specs = [pl.BlockSpec((tm, k), lambda j, i: (i, 0)),
                pl.BlockSpec((None, k, tn), lambda j, i: (layer, 0, j))]
    args = [x, w]
    if residual is not None:
        in_specs.append(pl.BlockSpec((tm, tn), lambda j, i: (i, j)))
        args.append(residual)
    return pl.pallas_call(
        functools.partial(_proj_kernel, has_res=residual is not None),
        grid=(n // tn, m // tm),
        in_specs=in_specs,
        out_specs=pl.BlockSpec((tm, tn), lambda j, i: (i, j)),
        out_shape=jax.ShapeDtypeStruct((m, n), out_dtype),
        scratch_shapes=[pltpu.VMEM((k, tn), BF16)],
        compiler_params=_params("arbitrary", "arbitrary"),
        name="proj",
    )(*args)


def _ffn_in_kernel(x_ref, wg_ref, wu_ref, cwg_ref, cwu_ref, bg_ref, bu_ref, o_ref,
                   wb_ref, hbuf_ref, *, tm, tn, ni, tiles_per_seq, chunk, mslab):
    s = pl.program_id(0)

    @pl.when(s == 0)
    def _():
        hbuf_ref[...] = jnp.zeros(hbuf_ref.shape, F32)

    @pl.when(s % ni == 0)
    def _():
        wb_ref[:, :tn] = wg_ref[...].astype(BF16)
        wb_ref[:, tn:] = wu_ref[...].astype(BF16)

    @pl.when((jnp.maximum(s - 1, 0) % ni) % tiles_per_seq == 0)
    def _():
        hbuf_ref[0:8, :] = jnp.zeros((8, 2 * tn), F32)

    cw = jnp.concatenate([cwg_ref[...], cwu_ref[...]], axis=1)
    bias = jnp.concatenate([bg_ref[...], bu_ref[...]], axis=1)
    w0, w1, w2 = cw[0:1], cw[1:2], cw[2:3]

    def conv_gate(r0, r1):
        for r in range(r0, r1, chunk):
            h0 = hbuf_ref[8 + r:8 + r + chunk, :]
            h1 = hbuf_ref[7 + r:7 + r + chunk, :]
            h2 = hbuf_ref[6 + r:6 + r + chunk, :]
            y = ((bias + w0 * h2) + w1 * h1) + w2 * h0
            gate, up = y[:, :tn], y[:, tn:]
            act = (gate * jax.nn.sigmoid(gate)) * up
            o_ref[r:r + chunk, :] = act.astype(o_ref.dtype)

    conv_gate(0, mslab)
    for c in range(tm // mslab):
        r = c * mslab
        if r + mslab < tm:
            conv_gate(r + mslab, r + 2 * mslab)
        else:
            hbuf_ref[0:8, :] = hbuf_ref[tm:tm + 8, :]
        hbuf_ref[8 + r:8 + r + mslab, :] = jnp.dot(x_ref[r:r + mslab, :], wb_ref[...],
                                                   preferred_element_type=F32)


def ffn_in(x, w_in, conv_w, conv_b, layer, seq, tm=1024, tn=256, chunk=64, mslab=128):
    m, k = x.shape
    d_ff = w_in.shape[-1] // 2
    nj = d_ff // tn
    ni = m // tm
    last = nj * ni - 1
    conv_b3 = conv_b.reshape(conv_b.shape[0], 1, 2 * d_ff)

    def mm_col(s):
        return jnp.minimum(s, last) // ni

    def ep_col(s):
        return jnp.maximum(s - 1, 0) // ni

    return pl.pallas_call(
        functools.partial(_ffn_in_kernel, tm=tm, tn=tn, ni=ni, tiles_per_seq=seq // tm,
                          chunk=chunk, mslab=mslab),
        grid=(nj * ni + 1,),
        in_specs=[pl.BlockSpec((tm, k), lambda s: (jnp.minimum(s, last) % ni, 0)),
                  pl.BlockSpec((None, k, tn), lambda s: (layer, 0, mm_col(s))),
                  pl.BlockSpec((None, k, tn), lambda s: (layer, 0, mm_col(s) + nj)),
                  pl.BlockSpec((None, CONV_WIDTH, tn), lambda s: (layer, 0, ep_col(s))),
                  pl.BlockSpec((None, CONV_WIDTH, tn), lambda s: (layer, 0, ep_col(s) + nj)),
                  pl.BlockSpec((None, 1, tn), lambda s: (layer, 0, ep_col(s))),
                  pl.BlockSpec((None, 1, tn), lambda s: (layer, 0, ep_col(s) + nj))],
        out_specs=pl.BlockSpec((tm, tn), lambda s: (jnp.maximum(s - 1, 0) % ni, ep_col(s))),
        out_shape=jax.ShapeDtypeStruct((m, d_ff), BF16),
        scratch_shapes=[pltpu.VMEM((k, 2 * tn), BF16),
                        pltpu.VMEM((tm + 8, 2 * tn), F32)],
        compiler_params=_params("arbitrary"),
        name="ffn_in",
    )(x, w_in, w_in, conv_w, conv_w, conv_b3, conv_b3)


def _ffn_down_kernel(x_ref, w_ref, r_ref, o_ref):
    @pl.when(pl.program_id(2) == 0)
    def _():
        o_ref[...] = r_ref[...]

    o_ref[...] += jnp.dot(x_ref[...], w_ref[...].astype(BF16), preferred_element_type=F32)


def ffn_down(x, w, layer, residual, tm=2048, tn=1024, tk=1024):
    m, k = x.shape
    n = w.shape[-1]
    return pl.pallas_call(
        _ffn_down_kernel,
        grid=(m // tm, n // tn, k // tk),
        in_specs=[pl.BlockSpec((tm, tk), lambda i, j, kk: (i, kk)),
                  pl.BlockSpec((None, tk, tn), lambda i, j, kk: (layer, kk, j)),
                  pl.BlockSpec((tm, tn), lambda i, j, kk: (i, j))],
        out_specs=pl.BlockSpec((tm, tn), lambda i, j, kk: (i, j)),
        out_shape=jax.ShapeDtypeStruct((m, n), F32),
        compiler_params=_params("arbitrary", "arbitrary", "arbitrary"),
        name="ffn_down",
    )(x, w, residual)


def _sb_kernel(q_ref, k_ref, v_ref, tri_ref, o_ref, c0_ref, acc0_ref, z_ref, hi_ref, zs_ref,
               cum_ref, w_ref, sp0_ref, r_ref, c_ref, acc_ref, *, seq, tile, group, scale):
    t = tile
    t2 = 2 * t
    npairs = seq // t2
    d = SB_HEAD_DIM
    tri = tri_ref[...]
    row = lax.broadcasted_iota(jnp.int32, (t, t), 0)
    col = lax.broadcasted_iota(jnp.int32, (t, t), 1)
    strict = col < row
    sign = jnp.uint32(0x80000000)

    heads = list(range(group))

    def key_tile(ref, g, kb):
        return ref[pl.ds(pl.multiple_of(kb * t, t), t), g * d:(g + 1) * d]

    def logits(qs, gs, kbs):
        return [lax.dot_general(q, key_tile(k_ref, g, kb), (((1,), (1,)), ((), ())),
                                preferred_element_type=F32)
                for q, g, kb in zip(qs, gs, kbs)]

    def suffix_sums(z_raw, masked):
        out = []
        zs = [z * (scale * LOG2E) for z in z_raw]
        sps = []
        for z in zs:
            neg_abs = lax.bitcast_convert_type(lax.bitcast_convert_type(z, jnp.uint32) | sign, F32)
            sp = jnp.maximum(z, 0.0) + jnp.log2(1.0 + jnp.exp2(neg_abs))
            sps.append(jnp.where(strict, sp, 0.0) if masked else sp)
        for z, sp in zip(zs, sps):
            cum = jnp.dot(sp.astype(BF16), tri, preferred_element_type=F32)
            rowsum = cum[:, 0:1] + sp[:, 0:1]
            if masked:
                cum = jnp.where(strict, cum, -NEG_BIG)
            out.append((z - sp, cum, rowsum))
        return out

    def accumulate(parts, gs, kbs, cs, accs):
        ws = [jnp.exp2((p[0] - p[1]) - c).astype(BF16) for p, c in zip(parts, cs)]
        accs = [a + jnp.dot(w, key_tile(v_ref, g, kb), preferred_element_type=F32)
                for a, w, g, kb in zip(accs, ws, gs, kbs)]
        cs = [c + p[2] for c, p in zip(cs, parts)]
        return cs, accs

    def visit(qs, gs, kbs, masked, cs, accs):
        return accumulate(suffix_sums(logits(qs, gs, kbs), masked), gs, kbs, cs, accs)

    def diag_body(qp, carry):
        q0 = 2 * qp
        q1 = q0 + 1
        rows = pl.ds(pl.multiple_of(qp * t2, t2), t2)
        qboth = [q_ref[rows, g * d:(g + 1) * d] for g in heads]
        qa = [q[:t] for q in qboth]
        qb = [q[t:] for q in qboth]
        zc = [jnp.zeros((t, 1), F32)] * (2 * group)
        za = [jnp.zeros((t, d), F32)] * (2 * group)
        cs, accs = visit(qa + qb, heads + heads, [q0] * group + [q1] * group, True, zc, za)
        cb, ab = visit(qb, heads, [q0] * group, False, cs[group:], accs[group:])
        for g in heads:
            acc = jnp.concatenate([accs[g], ab[g]], axis=0)
            c0_ref[g, rows, :] = jnp.concatenate([cs[g], cb[g]], axis=0)
            acc0_ref[rows, g * d:(g + 1) * d] = acc
            o_ref[rows, g * d:(g + 1) * d] = acc.astype(o_ref.dtype)
        return carry

    lax.fori_loop(0, npairs, diag_body, 0)

    for buf in (z_ref, hi_ref, zs_ref, cum_ref, w_ref, sp0_ref, r_ref, c_ref, acc_ref):
        buf[...] = jnp.zeros(buf.shape, buf.dtype)

    def step(i, hist):
        (q1, k1), (q2, k2), (q3, k3), (q4, k4), (q5, k5) = hist
        slot = i % 2

        rows5 = pl.ds(pl.multiple_of(q5 * t2, t2), t2)
        first5 = k5 == 2 * q5 - 1
        for g in heads:
            a_in = jnp.where(first5, acc0_ref[rows5, g * d:(g + 1) * d], acc_ref[g])
            a = a_in + jnp.dot(w_ref[g], key_tile(v_ref, g, k5), preferred_element_type=F32)
            acc_ref[g] = a
            o_ref[rows5, g * d:(g + 1) * d] = a.astype(o_ref.dtype)

        rows4 = pl.ds(pl.multiple_of(q4 * t2, t2), t2)
        first4 = k4 == 2 * q4 - 1
        for g in heads:
            c_in = jnp.where(first4, c0_ref[g, rows4, :], c_ref[g])
            w_ref[g] = jnp.exp2((zs_ref[slot, g] - cum_ref[g]) - c_in).astype(BF16)
            c_ref[g] = c_in + r_ref[g]

        for g in heads:
            cum = jnp.dot(hi_ref[g], tri, preferred_element_type=F32)
            cum_ref[g] = cum
            r_ref[g] = cum[:, 0:1] + sp0_ref[g]

        for g in heads:
            z = z_ref[g] * (scale * LOG2E)
            neg_abs = lax.bitcast_convert_type(lax.bitcast_convert_type(z, jnp.uint32) | sign, F32)
            sp = jnp.maximum(z, 0.0) + jnp.log2(1.0 + jnp.exp2(neg_abs))
            hi_ref[g] = sp.astype(BF16)
            zs_ref[slot, g] = z - sp
            sp0_ref[g] = sp[:, 0:1]

        rows1 = pl.ds(pl.multiple_of(q1 * t2, t2), t2)
        for g in heads:
            z_ref[g] = lax.dot_general(q_ref[rows1, g * d:(g + 1) * d], key_tile(k_ref, g, k1),
                                       (((1,), (1,)), ((), ())), preferred_element_type=F32)

        wrap = k1 == 0
        qn = jnp.where(wrap, q1 + 1, q1)
        kn = jnp.where(wrap, 2 * qn - 1, k1 - 1)
        done = qn >= npairs
        qn = jnp.where(done, q1, qn)
        kn = jnp.where(done, k1, kn)
        return ((qn, kn), (q1, k1), (q2, k2), (q3, k3), (q4, k4))

    if npairs > 1:
        one = jnp.int32(1)
        lax.fori_loop(0, npairs * (npairs - 1) + 4, step, ((one, one),) * 5, unroll=2)


def sb_attention(qkv, batch, seq, tile=256, group=2):
    m = qkv.shape[0]
    heads = qkv.shape[1] // (3 * SB_HEAD_DIM)
    ng = heads // group
    d = SB_HEAD_DIM
    gw = group * d
    t2 = 2 * tile
    idx = jnp.arange(tile)
    tri = (idx[:, None] > idx[None, :]).astype(BF16)
    return pl.pallas_call(
        functools.partial(_sb_kernel, seq=seq, tile=tile, group=group, scale=d ** -0.5),
        grid=(batch, ng),
        in_specs=[pl.BlockSpec((seq, gw), lambda b, h: (b, h)),
                  pl.BlockSpec((seq, gw), lambda b, h: (b, ng + h)),
                  pl.BlockSpec((seq, gw), lambda b, h: (b, 2 * ng + h)),
                  pl.BlockSpec((tile, tile), lambda b, h: (0, 0))],
        out_specs=pl.BlockSpec((seq, gw), lambda b, h: (b, h)),
        out_shape=jax.ShapeDtypeStruct((m, heads * d), BF16),
        scratch_shapes=[pltpu.VMEM((group, seq, 1), F32),
                        pltpu.VMEM((seq, gw), F32),
                        pltpu.VMEM((group, t2, tile), F32),
                        pltpu.VMEM((group, t2, tile), BF16),
                        pltpu.VMEM((2, group, t2, tile), F32),
                        pltpu.VMEM((group, t2, tile), F32),
                        pltpu.VMEM((group, t2, tile), BF16),
                        pltpu.VMEM((group, t2, 1), F32),
                        pltpu.VMEM((group, t2, 1), F32),
                        pltpu.VMEM((group, t2, 1), F32),
                        pltpu.VMEM((group, t2, d), F32)],
        compiler_params=_params("arbitrary", "arbitrary"),
        name="sb_attention",
    )(qkv, qkv, qkv, tri)


def _swa_kernel(slopes_ref, sinks_ref, q_ref, k_ref, v_ref, o_ref, *, seq, scale):
    blk_sz = SWA_BLOCK
    kvh = pl.program_id(1)
    odd = (kvh % 2) == 1
    lane = lax.broadcasted_iota(jnp.int32, (2 * blk_sz, 128), 1)
    low = lane < SWA_HEAD_DIM

    q_idx = lax.broadcasted_iota(jnp.int32, (blk_sz, 2 * blk_sz), 0)
    k_idx = lax.broadcasted_iota(jnp.int32, (blk_sz, 2 * blk_sz), 1)
    dist = q_idx + blk_sz - k_idx
    in_window = (dist >= 0) & (dist < blk_sz)
    dist_f = dist.astype(F32)

    def place(band):
        x = band.astype(F32)
        x = jnp.where(odd, pltpu.roll(x, SWA_HEAD_DIM, 1), x)
        at0 = jnp.where(low, x, 0.0)
        at64 = pltpu.roll(at0, SWA_HEAD_DIM, 1)
        return at0.astype(BF16), at64.astype(BF16)

    def body(blk, carry):
        cur = pl.multiple_of(blk * blk_sz, blk_sz)
        prev = pl.multiple_of(jnp.maximum(blk - 1, 0) * blk_sz, blk_sz)
        kband = jnp.concatenate([k_ref[pl.ds(prev, blk_sz), :], k_ref[pl.ds(cur, blk_sz), :]], axis=0)
        vband = jnp.concatenate([v_ref[pl.ds(prev, blk_sz), :], v_ref[pl.ds(cur, blk_sz), :]], axis=0)
        k_at = place(kband)
        v_at = place(vband)
        valid = in_window & ((blk * blk_sz + k_idx - blk_sz) >= 0)
        for c in range(SWA_GROUP // 2):
            qc = q_ref[pl.ds(cur, blk_sz), c * 128:(c + 1) * 128]
            out = None
            for half in range(2):
                head = kvh * SWA_GROUP + 2 * c + half
                s = lax.dot_general(qc, k_at[half], (((1,), (1,)), ((), ())),
                                    preferred_element_type=F32)
                s = s * scale - slopes_ref[head] * dist_f
                s = jnp.where(valid, s, NEG_BIG)
                sink = sinks_ref[head]
                mx = jnp.maximum(jnp.max(s, axis=-1, keepdims=True), sink)
                p = jnp.exp(s - mx)
                denom = jnp.sum(p, axis=-1, keepdims=True) + jnp.exp(sink - mx)
                o = jnp.dot(p.astype(BF16), v_at[half], preferred_element_type=F32) / denom
                out = o if out is None else out + o
            o_ref[pl.ds(cur, blk_sz), c * 128:(c + 1) * 128] = out.astype(o_ref.dtype)
        return carry

    lax.fori_loop(0, seq // blk_sz, body, 0)


def swa_attention(qkv, slopes, sinks, batch, seq, heads):
    m = qkv.shape[0]
    kv_heads = heads // SWA_GROUP
    qd = heads * SWA_HEAD_DIM
    gw = SWA_GROUP * SWA_HEAD_DIM
    k_blk0 = qd // 128
    v_blk0 = (qd + kv_heads * SWA_HEAD_DIM) // 128
    smem = pl.BlockSpec(memory_space=pltpu.SMEM)
    return pl.pallas_call(
        functools.partial(_swa_kernel, seq=seq, scale=SWA_HEAD_DIM ** -0.5),
        grid=(batch, kv_heads),
        in_specs=[smem, smem,
                  pl.BlockSpec((seq, gw), lambda b, h: (b, h)),
                  pl.BlockSpec((seq, 128), lambda b, h: (b, k_blk0 + h // 2)),
                  pl.BlockSpec((seq, 128), lambda b, h: (b, v_blk0 + h // 2))],
        out_specs=pl.BlockSpec((seq, gw), lambda b, h: (b, h)),
        out_shape=jax.ShapeDtypeStruct((m, qd), BF16),
        compiler_params=_params("arbitrary", "arbitrary"),
        name="swa_attention",
    )(slopes, sinks, qkv, qkv, qkv)


def kernel(x, attn_norm, ffn_norm, sb_w_qkv, sb_w_o, swa_w_qkv, swa_w_o, swa_sinks,
           ffn_w_in, ffn_conv_w, ffn_conv_b, ffn_w_down, final_norm):
    batch, seq, d = x.shape
    depth = attn_norm.shape[0]
    swa_heads = swa_sinks.shape[-1]
    slopes = jnp.asarray(2.0 ** (-8.0 * np.arange(1, swa_heads + 1) / swa_heads), F32)
    h = x.reshape(batch * seq, d)
    ffn_in_tiles = [dict(), dict(mslab=256), dict(chunk=128), dict(mslab=256, chunk=128)]
    ffn_down_tiles = [dict(), dict(tm=1024, tn=1024, tk=2048), dict(tm=2048, tn=512, tk=2048),
                      dict(tm=1024, tn=2048, tk=1024)]
    qkv_tiles = [dict(), dict(tm=512)]
    for i in range(depth):
        xn = rmsnorm(h, attn_norm, i, BF16)
        j = i // 2
        if i % 2 == 0:
            qkv = proj(xn, sb_w_qkv, j, BF16, **qkv_tiles[j])
            mix = sb_attention(qkv, batch, seq)
            h = proj(mix, sb_w_o, j, F32, residual=h)
        else:
            qkv = proj(xn, swa_w_qkv, j, BF16)
            mix = swa_attention(qkv, slopes, swa_sinks[j], batch, seq, swa_heads)
            h = proj(mix, swa_w_o, j, F32, residual=h)
        xn = rmsnorm(h, ffn_norm, i, BF16)
        act = ffn_in(xn, ffn_w_in, ffn_conv_w, ffn_conv_b, i, seq, **ffn_in_tiles[i])
        h = ffn_down(act, ffn_w_down, i, h, **ffn_down_tiles[i])
    out = rmsnorm(h, final_norm.reshape(1, d), 0, F32)
    return out.reshape(batch, seq, d)
```

```python
import functools

import jax
import jax.numpy as jnp
import numpy as np
from jax import lax
from jax.experimental import pallas as pl
from jax.experimental.pallas import tpu as pltpu

F32 = jnp.float32
BF16 = jnp.bfloat16

RMS_EPS = 1e-6
SB_HEAD_DIM = 128
SWA_HEAD_DIM = 64
SWA_GROUP = 8
SWA_BLOCK = 128
CONV_WIDTH = 3
NEG_BIG = -1e30
LOG2E = 1.4426950408889634
SB_DONE = 192.0

VMEM_LIMIT = 56 * 1024 * 1024


def _params(*sem):
    return pltpu.CompilerParams(dimension_semantics=sem, vmem_limit_bytes=VMEM_LIMIT)


def _rmsnorm_kernel(x_ref, g_ref, o_ref):
    x = x_ref[...]
    ms = jnp.mean(x * x, axis=-1, keepdims=True)
    o_ref[...] = ((x * lax.rsqrt(ms + RMS_EPS)) * g_ref[...]).astype(o_ref.dtype)


def rmsnorm(x, gain, layer, out_dtype, tm=256):
    m, d = x.shape
    gain3 = gain.reshape(-1, 1, d)
    return pl.pallas_call(
        _rmsnorm_kernel,
        grid=(m // tm,),
        in_specs=[pl.BlockSpec((tm, d), lambda i: (i, 0)),
                  pl.BlockSpec((None, 1, d), lambda i: (layer, 0, 0))],
        out_specs=pl.BlockSpec((tm, d), lambda i: (i, 0)),
        out_shape=jax.ShapeDtypeStruct((m, d), out_dtype),
        compiler_params=_params("arbitrary"),
        name="rmsnorm",
    )(x, gain3)


def _proj_kernel(x_ref, w_ref, *rest, has_res):
    if has_res:
        r_ref, o_ref, wb_ref = rest
    else:
        o_ref, wb_ref = rest

    @pl.when(pl.program_id(1) == 0)
    def _():
        wb_ref[...] = w_ref[...].astype(BF16)

    acc = jnp.dot(x_ref[...], wb_ref[...], preferred_element_type=F32)
    if has_res:
        acc = acc + r_ref[...]
    o_ref[...] = acc.astype(o_ref.dtype)


def proj(x, w, layer, out_dtype, residual=None, tm=1024, tn=512):
    m, k = x.shape
    n = w.shape[-1]
    in_specs = [pl.BlockSpec((tm, k), lambda j, i: (i, 0)),
                pl.BlockSpec((None, k, tn), lambda j, i: (layer, 0, j))]
    args = [x, w]
    if residual is not None:
        in_specs.append(pl.BlockSpec((tm, tn), lambda j, i: (i, j)))
        args.append(residual)
    return pl.pallas_call(
        functools.partial(_proj_kernel, has_res=residual is not None),
        grid=(n // tn, m // tm),
        in_specs=in_specs,
        out_specs=pl.BlockSpec((tm, tn), lambda j, i: (i, j)),
        out_shape=jax.ShapeDtypeStruct((m, n), out_dtype),
        scratch_shapes=[pltpu.VMEM((k, tn), BF16)],
        compiler_params=_params("arbitrary", "arbitrary"),
        name="proj",
    )(*args)


def _ffn_in_kernel(x_ref, wg_ref, wu_ref, cwg_ref, cwu_ref, bg_ref, bu_ref, o_ref,
                   wb_ref, hbuf_ref, *, tm, tn, ni, tiles_per_seq, chunk, mslab):
    s = pl.program_id(0)

    @pl.when(s == 0)
    def _():
        hbuf_ref[...] = jnp.zeros(hbuf_ref.shape, F32)

    @pl.when(s % ni == 0)
    def _():
        wb_ref[:, :tn] = wg_ref[...].astype(BF16)
        wb_ref[:, tn:] = wu_ref[...].astype(BF16)

    @pl.when((jnp.maximum(s - 1, 0) % ni) % tiles_per_seq == 0)
    def _():
        hbuf_ref[0:8, :] = jnp.zeros((8, 2 * tn), F32)

    cw = jnp.concatenate([cwg_ref[...], cwu_ref[...]], axis=1)
    bias = jnp.concatenate([bg_ref[...], bu_ref[...]], axis=1)
    w0, w1, w2 = cw[0:1], cw[1:2], cw[2:3]

    def conv_gate(r0, r1):
        for r in range(r0, r1, chunk):
            h0 = hbuf_ref[8 + r:8 + r + chunk, :]
            h1 = hbuf_ref[7 + r:7 + r + chunk, :]
            h2 = hbuf_ref[6 + r:6 + r + chunk, :]
            y = ((bias + w0 * h2) + w1 * h1) + w2 * h0
            gate, up = y[:, :tn], y[:, tn:]
            act = (gate * jax.nn.sigmoid(gate)) * up
            o_ref[r:r + chunk, :] = act.astype(o_ref.dtype)

    conv_gate(0, mslab)
    for c in range(tm // mslab):
        r = c * mslab
        if r + mslab < tm:
            conv_gate(r + mslab, r + 2 * mslab)
        else:
            hbuf_ref[0:8, :] = hbuf_ref[tm:tm + 8, :]
        hbuf_ref[8 + r:8 + r + mslab, :] = jnp.dot(x_ref[r:r + mslab, :], wb_ref[...],
                                                   preferred_element_type=F32)


def ffn_in(x, w_in, conv_w, conv_b, layer, seq, tm=1024, tn=256, chunk=128, mslab=128):
    m, k = x.shape
    d_ff = w_in.shape[-1] // 2
    nj = d_ff // tn
    ni = m // tm
    last = nj * ni - 1
    conv_b3 = conv_b.reshape(conv_b.shape[0], 1, 2 * d_ff)

    def mm_col(s):
        return jnp.minimum(s, last) // ni

    def ep_col(s):
        return jnp.maximum(s - 1, 0) // ni

    return pl.pallas_call(
        functools.partial(_ffn_in_kernel, tm=tm, tn=tn, ni=ni, tiles_per_seq=seq // tm,
                          chunk=chunk, mslab=mslab),
        grid=(nj * ni + 1,),
        in_specs=[pl.BlockSpec((tm, k), lambda s: (jnp.minimum(s, last) % ni, 0)),
                  pl.BlockSpec((None, k, tn), lambda s: (layer, 0, mm_col(s))),
                  pl.BlockSpec((None, k, tn), lambda s: (layer, 0, mm_col(s) + nj)),
                  pl.BlockSpec((None, CONV_WIDTH, tn), lambda s: (layer, 0, ep_col(s))),
                  pl.BlockSpec((None, CONV_WIDTH, tn), lambda s: (layer, 0, ep_col(s) + nj)),
                  pl.BlockSpec((None, 1, tn), lambda s: (layer, 0, ep_col(s))),
                  pl.BlockSpec((None, 1, tn), lambda s: (layer, 0, ep_col(s) + nj))],
        out_specs=pl.BlockSpec((tm, tn), lambda s: (jnp.maximum(s - 1, 0) % ni, ep_col(s))),
        out_shape=jax.ShapeDtypeStruct((m, d_ff), BF16),
        scratch_shapes=[pltpu.VMEM((k, 2 * tn), BF16),
                        pltpu.VMEM((tm + 8, 2 * tn), F32)],
        compiler_params=_params("arbitrary"),
        name="ffn_in",
    )(x, w_in, w_in, conv_w, conv_w, conv_b3, conv_b3)


def _ffn_down_kernel(x_ref, w_ref, r_ref, o_ref):
    @pl.when(pl.program_id(2) == 0)
    def _():
        o_ref[...] = r_ref[...]

    o_ref[...] += jnp.dot(x_ref[...], w_ref[...].astype(BF16), preferred_element_type=F32)


def ffn_down(x, w, layer, residual, tm=2048, tn=1024, tk=1024):
    m, k = x.shape
    n = w.shape[-1]
    return pl.pallas_call(
        _ffn_down_kernel,
        grid=(m // tm, n // tn, k // tk),
        in_specs=[pl.BlockSpec((tm, tk), lambda i, j, kk: (i, kk)),
                  pl.BlockSpec((None, tk, tn), lambda i, j, kk: (layer, kk, j)),
                  pl.BlockSpec((tm, tn), lambda i, j, kk: (i, j))],
        out_specs=pl.BlockSpec((tm, tn), lambda i, j, kk: (i, j)),
        out_shape=jax.ShapeDtypeStruct((m, n), F32),
        compiler_params=_params("arbitrary", "arbitrary", "arbitrary"),
        name="ffn_down",
    )(x, w, residual)


def _sb_kernel(q_ref, k_ref, v_ref, tri_ref, o_ref, todo_ref, c0_ref, acc0_ref, z_ref, hi_ref,
               zs_ref, cum_ref, w_ref, sp0_ref, r_ref, c_ref, acc_ref, *, seq, tile, group, scale):
    t = tile
    t2 = 2 * t
    npairs = seq // t2
    d = SB_HEAD_DIM
    tri = tri_ref[...]
    row = lax.broadcasted_iota(jnp.int32, (t, t), 0)
    col = lax.broadcasted_iota(jnp.int32, (t, t), 1)
    strict = col < row
    sign = jnp.uint32(0x80000000)

    heads = list(range(group))

    def key_tile(ref, g, kb):
        return ref[pl.ds(pl.multiple_of(kb * t, t), t), g * d:(g + 1) * d]

    def logits(qs, gs, kbs):
        return [lax.dot_general(q, key_tile(k_ref, g, kb), (((1,), (1,)), ((), ())),
                                preferred_element_type=F32)
                for q, g, kb in zip(qs, gs, kbs)]

    def suffix_sums(z_raw, masked):
        out = []
        zs = [z * (scale * LOG2E) for z in z_raw]
        sps = []
        for z in zs:
            neg_abs = lax.bitcast_convert_type(lax.bitcast_convert_type(z, jnp.uint32) | sign, F32)
            sp = jnp.maximum(z, 0.0) + jnp.log2(1.0 + jnp.exp2(neg_abs))
            sps.append(jnp.where(strict, sp, 0.0) if masked else sp)
        for z, sp in zip(zs, sps):
            cum = jnp.dot(sp.astype(BF16), tri, preferred_element_type=F32)
            rowsum = cum[:, 0:1] + sp[:, 0:1]
            if masked:
                cum = jnp.where(strict, cum, -NEG_BIG)
            out.append((z - sp, cum, rowsum))
        return out

    def accumulate(parts, gs, kbs, cs, accs):
        ws = [jnp.exp2((p[0] - p[1]) - c).astype(BF16) for p, c in zip(parts, cs)]
        accs = [a + jnp.dot(w, key_tile(v_ref, g, kb), preferred_element_type=F32)
                for a, w, g, kb in zip(accs, ws, gs, kbs)]
        cs = [c + p[2] for c, p in zip(cs, parts)]
        return cs, accs

    def visit(qs, gs, kbs, masked, cs, accs):
        return accumulate(suffix_sums(logits(qs, gs, kbs), masked), gs, kbs, cs, accs)

    def diag_body(qp, carry):
        cnt, total = carry
        q0 = 2 * qp
        q1 = q0 + 1
        has_prev = qp >= 1
        kprev = jnp.maximum(q0 - 1, 0)
        rows = pl.ds(pl.multiple_of(qp * t2, t2), t2)
        qboth = [q_ref[rows, g * d:(g + 1) * d] for g in heads]
        qa = [q[:t] for q in qboth]
        qb = [q[t:] for q in qboth]
        zc = [jnp.zeros((t, 1), F32)] * (2 * group)
        za = [jnp.zeros((t, d), F32)] * (2 * group)
        cs, accs = visit(qa + qb, heads + heads, [q0] * group + [q1] * group, True, zc, za)
        cs2, accs2 = visit(qb + qa, heads + heads, [q0] * group + [kprev] * group, False,
                           cs[group:] + cs[:group], accs[group:] + accs[:group])
        cb3, ab3 = visit(qb, heads, [kprev] * group, False, cs2[:group], accs2[:group])
        cmin = None
        for g in heads:
            ca = jnp.where(has_prev, cs2[group + g], cs[g])
            aa = jnp.where(has_prev, accs2[group + g], accs[g])
            cb = jnp.where(has_prev, cb3[g], cs2[g])
            ab = jnp.where(has_prev, ab3[g], accs2[g])
            acc = jnp.concatenate([aa, ab], axis=0)
            c0_ref[g, rows, :] = jnp.concatenate([ca, cb], axis=0)
            acc0_ref[rows, g * d:(g + 1) * d] = acc
            o_ref[rows, g * d:(g + 1) * d] = acc.astype(o_ref.dtype)
            m = jnp.minimum(ca, cb)
            cmin = m if cmin is None else jnp.minimum(cmin, m)
        more = has_prev & (jnp.min(cmin) < SB_DONE)

        @pl.when(more)
        def _():
            todo_ref[cnt] = qp

        more_i = more.astype(jnp.int32)
        return cnt + more_i, total + more_i * (2 * qp - 1)

    todo_ref[0] = jnp.int32(1)
    cnt, total = lax.fori_loop(0, npairs, diag_body, (jnp.int32(0), jnp.int32(0)))

    for buf in (z_ref, hi_ref, zs_ref, cum_ref, w_ref, sp0_ref, r_ref, c_ref, acc_ref):
        buf[...] = jnp.zeros(buf.shape, buf.dtype)

    def listed(li):
        return todo_ref[jnp.clip(li, 0, jnp.maximum(cnt - 1, 0))]

    def step(i, state):
        li, hist = state
        (q1, k1), (q2, k2), (q3, k3), (q4, k4), (q5, k5) = hist
        slot = i % 2

        rows5 = pl.ds(pl.multiple_of(q5 * t2, t2), t2)
        first5 = k5 == 2 * q5 - 2
        for g in heads:
            a_in = jnp.where(first5, acc0_ref[rows5, g * d:(g + 1) * d], acc_ref[g])
            a = a_in + jnp.dot(w_ref[g], key_tile(v_ref, g, k5), preferred_element_type=F32)
            acc_ref[g] = a
            o_ref[rows5, g * d:(g + 1) * d] = a.astype(o_ref.dtype)

        rows4 = pl.ds(pl.multiple_of(q4 * t2, t2), t2)
        first4 = k4 == 2 * q4 - 2
        for g in heads:
            c_in = jnp.where(first4, c0_ref[g, rows4, :], c_ref[g])
            w_ref[g] = jnp.exp2((zs_ref[slot, g] - cum_ref[g]) - c_in).astype(BF16)
            c_ref[g] = c_in + r_ref[g]

        for g in heads:
            cum = jnp.dot(hi_ref[g], tri, preferred_element_type=F32)
            cum_ref[g] = cum
            r_ref[g] = cum[:, 0:1] + sp0_ref[g]

        for g in heads:
            z = z_ref[g] * (scale * LOG2E)
            neg_abs = lax.bitcast_convert_type(lax.bitcast_convert_type(z, jnp.uint32) | sign, F32)
            sp = jnp.maximum(z, 0.0) + jnp.log2(1.0 + jnp.exp2(neg_abs))
            hi_ref[g] = sp.astype(BF16)
            zs_ref[slot, g] = z - sp
            sp0_ref[g] = sp[:, 0:1]

        rows1 = pl.ds(pl.multiple_of(q1 * t2, t2), t2)
        for g in heads:
            z_ref[g] = lax.dot_general(q_ref[rows1, g * d:(g + 1) * d], key_tile(k_ref, g, k1),
                                       (((1,), (1,)), ((), ())), preferred_element_type=F32)

        wrap = k1 == 0
        done = wrap & (li + 1 >= cnt)
        li_n = jnp.where(wrap & ~done, li + 1, li)
        q_w = listed(li_n)
        qn = jnp.where(done, q1, jnp.where(wrap, q_w, q1))
        kn = jnp.where(done, k1, jnp.where(wrap, 2 * q_w - 2, k1 - 1))
        return li_n, ((qn, kn), (q1, k1), (q2, k2), (q3, k3), (q4, k4))

    qf = listed(0)
    steps = jnp.where(total > 0, total + 4, 0)
    lax.fori_loop(0, steps, step, (jnp.int32(0), ((qf, 2 * qf - 2),) * 5))


def sb_attention(qkv, batch, seq, tile=256, group=2):
    m = qkv.shape[0]
    heads = qkv.shape[1] // (3 * SB_HEAD_DIM)
    ng = heads // group
    d = SB_HEAD_DIM
    gw = group * d
    t2 = 2 * tile
    idx = jnp.arange(tile)
    tri = (idx[:, None] > idx[None, :]).astype(BF16)
    return pl.pallas_call(
        functools.partial(_sb_kernel, seq=seq, tile=tile, group=group, scale=d ** -0.5),
        grid=(batch, ng),
        in_specs=[pl.BlockSpec((seq, gw), lambda b, h: (b, h)),
                  pl.BlockSpec((seq, gw), lambda b, h: (b, ng + h)),
                  pl.BlockSpec((seq, gw), lambda b, h: (b, 2 * ng + h)),
                  pl.BlockSpec((tile, tile), lambda b, h: (0, 0))],
        out_specs=pl.BlockSpec((seq, gw), lambda b, h: (b, h)),
        out_shape=jax.ShapeDtypeStruct((m, heads * d), BF16),
        scratch_shapes=[pltpu.SMEM((seq // t2,), jnp.int32),
                        pltpu.VMEM((group, seq, 1), F32),
                        pltpu.VMEM((seq, gw), F32),
                        pltpu.VMEM((group, t2, tile), F32),
                        pltpu.VMEM((group, t2, tile), BF16),
                        pltpu.VMEM((2, group, t2, tile), F32),
                        pltpu.VMEM((group, t2, tile), F32),
                        pltpu.VMEM((group, t2, tile), BF16),
                        pltpu.VMEM((group, t2, 1), F32),
                        pltpu.VMEM((group, t2, 1), F32),
                        pltpu.VMEM((group, t2, 1), F32),
                        pltpu.VMEM((group, t2, d), F32)],
        compiler_params=_params("arbitrary", "arbitrary"),
        name="sb_attention",
    )(qkv, qkv, qkv, tri)


def _swa_kernel(slopes_ref, sinks_ref, q_ref, k_ref, v_ref, o_ref, *, seq, scale):
    blk_sz = SWA_BLOCK
    kvh = pl.program_id(1)
    odd = (kvh % 2) == 1
    lane = lax.broadcasted_iota(jnp.int32, (2 * blk_sz, 128), 1)
    low = lane < SWA_HEAD_DIM

    r_idx = lax.broadcasted_iota(jnp.int32, (blk_sz, blk_sz), 0)
    j_idx = lax.broadcasted_iota(jnp.int32, (blk_sz, blk_sz), 1)
    from_prev = j_idx > r_idx
    dist_f = jnp.where(from_prev, r_idx + blk_sz - j_idx, r_idx - j_idx).astype(F32)

    def place(band):
        x = band.astype(F32)
        x = jnp.where(odd, pltpu.roll(x, SWA_HEAD_DIM, 1), x)
        at0 = jnp.where(low, x, 0.0)
        at64 = pltpu.roll(at0, SWA_HEAD_DIM, 1)
        return at0.astype(BF16), at64.astype(BF16)

    def body(blk, carry):
        cur = pl.multiple_of(blk * blk_sz, blk_sz)
        prev = pl.multiple_of(jnp.maximum(blk - 1, 0) * blk_sz, blk_sz)
        kband = jnp.concatenate([k_ref[pl.ds(prev, blk_sz), :], k_ref[pl.ds(cur, blk_sz), :]], axis=0)
        vband = jnp.concatenate([v_ref[pl.ds(prev, blk_sz), :], v_ref[pl.ds(cur, blk_sz), :]], axis=0)
        k_at = place(kband)
        v_at = place(vband)
        masked = from_prev & (blk == 0)
        for c in range(SWA_GROUP // 2):
            qc = q_ref[pl.ds(cur, blk_sz), c * 128:(c + 1) * 128]
            out = None
            for half in range(2):
                head = kvh * SWA_GROUP + 2 * c + half
                s2 = lax.dot_general(qc, k_at[half], (((1,), (1,)), ((), ())),
                                     preferred_element_type=F32)
                s = jnp.where(from_prev, s2[:, :blk_sz], s2[:, blk_sz:])
                s = s * scale - slopes_ref[head] * dist_f
                s = jnp.where(masked, NEG_BIG, s)
                sink = sinks_ref[head]
                mx = jnp.maximum(jnp.max(s, axis=-1, keepdims=True), sink)
                p = jnp.exp(s - mx)
                denom = jnp.sum(p, axis=-1, keepdims=True) + jnp.exp(sink - mx)
                p2 = jnp.concatenate([jnp.where(from_prev, p, 0.0), jnp.where(from_prev, 0.0, p)],
                                     axis=1)
                o = jnp.dot(p2.astype(BF16), v_at[half], preferred_element_type=F32) / denom
                out = o if out is None else out + o
            o_ref[pl.ds(cur, blk_sz), c * 128:(c + 1) * 128] = out.astype(o_ref.dtype)
        return carry

    lax.fori_loop(0, seq // blk_sz, body, 0)


def swa_attention(qkv, slopes, sinks, batch, seq, heads):
    m = qkv.shape[0]
    kv_heads = heads // SWA_GROUP
    qd = heads * SWA_HEAD_DIM
    gw = SWA_GROUP * SWA_HEAD_DIM
    k_blk0 = qd // 128
    v_blk0 = (qd + kv_heads * SWA_HEAD_DIM) // 128
    smem = pl.BlockSpec(memory_space=pltpu.SMEM)
    return pl.pallas_call(
        functools.partial(_swa_kernel, seq=seq, scale=SWA_HEAD_DIM ** -0.5),
        grid=(batch, kv_heads),
        in_specs=[smem, smem,
                  pl.BlockSpec((seq, gw), lambda b, h: (b, h)),
                  pl.BlockSpec((seq, 128), lambda b, h: (b, k_blk0 + h // 2)),
                  pl.BlockSpec((seq, 128), lambda b, h: (b, v_blk0 + h // 2))],
        out_specs=pl.BlockSpec((seq, gw), lambda b, h: (b, h)),
        out_shape=jax.ShapeDtypeStruct((m, qd), BF16),
        compiler_params=_params("arbitrary", "arbitrary"),
        name="swa_attention",
    )(slopes, sinks, qkv, qkv, qkv)


def kernel(x, attn_norm, ffn_norm, sb_w_qkv, sb_w_o, swa_w_qkv, swa_w_o, swa_sinks,
           ffn_w_in, ffn_conv_w, ffn_conv_b, ffn_w_down, final_norm):
    batch, seq, d = x.shape
    depth = attn_norm.shape[0]
    swa_heads = swa_sinks.shape[-1]
    slopes = jnp.asarray(2.0 ** (-8.0 * np.arange(1, swa_heads + 1) / swa_heads), F32)
    h = x.reshape(batch * seq, d)
    for i in range(depth):
        xn = rmsnorm(h, attn_norm, i, BF16)
        j = i // 2
        if i % 2 == 0:
            qkv = proj(xn, sb_w_qkv, j, BF16)
            mix = sb_attention(qkv, batch, seq)
            h = proj(mix, sb_w_o, j, F32, residual=h)
        else:
            qkv = proj(xn, swa_w_qkv, j, BF16)
            mix = swa_attention(qkv, slopes, swa_sinks[j], batch, seq, swa_heads)
            h = proj(mix, swa_w_o, j, F32, residual=h)
        xn = rmsnorm(h, ffn_norm, i, BF16, tm=512 if i % 2 else 128)
        act = ffn_in(xn, ffn_w_in, ffn_conv_w, ffn_conv_b, i, seq,
                     **(dict(tm=512, tn=512) if i >= 2 else {}))
        h = ffn_down(act, ffn_w_down, i, h)
    out = rmsnorm(h, final_norm.reshape(1, d), 0, F32)
    return out.reshape(batch, seq, d)
```

```python
import functools

import jax
import jax.numpy as jnp
import numpy as np
from jax import lax
from jax.experimental import pallas as pl
from jax.experimental.pallas import tpu as pltpu

F32 = jnp.float32
BF16 = jnp.bfloat16

RMS_EPS = 1e-6
SB_HEAD_DIM = 128
SWA_HEAD_DIM = 64
SWA_GROUP = 8
SWA_BLOCK = 128
CONV_WIDTH = 3
NEG_BIG = -1e30
LOG2E = 1.4426950408889634
SB_DONE = 192.0

V7X_VMEM_BYTES = 64 * 1024 * 1024
VMEM_LIMIT = V7X_VMEM_BYTES * 7 // 8
HALO = 8
LANES = 128


def _params(*sem):
    return pltpu.CompilerParams(dimension_semantics=sem, vmem_limit_bytes=VMEM_LIMIT)


def _rmsnorm_kernel(x_ref, g_ref, o_ref):
    x = x_ref[...]
    ms = jnp.mean(x * x, axis=-1, keepdims=True)
    o_ref[...] = ((x * lax.rsqrt(ms + RMS_EPS)) * g_ref[...]).astype(o_ref.dtype)


def rmsnorm(x, gain, layer, out_dtype, tm=512):
    m, d = x.shape
    assert m % tm == 0
    gain3 = gain.reshape(-1, 1, d)
    return pl.pallas_call(
        _rmsnorm_kernel,
        grid=(m // tm,),
        in_specs=[pl.BlockSpec((tm, d), lambda i: (i, 0)),
                  pl.BlockSpec((None, 1, d), lambda i: (layer, 0, 0))],
        out_specs=pl.BlockSpec((tm, d), lambda i: (i, 0)),
        out_shape=jax.ShapeDtypeStruct((m, d), out_dtype),
        compiler_params=_params("arbitrary"),
        name="rmsnorm",
    )(x, gain3)


def _proj_kernel(x_ref, w_ref, *rest, has_res):
    if has_res:
        r_ref, o_ref, wb_ref = rest
    else:
        o_ref, wb_ref = rest

    @pl.when(pl.program_id(1) == 0)
    def _():
        wb_ref[...] = w_ref[...].astype(BF16)

    acc = jnp.dot(x_ref[...], wb_ref[...], preferred_element_type=F32)
    if has_res:
        acc = acc + r_ref[...]
    o_ref[...] = acc.astype(o_ref.dtype)


def proj(x, w, layer, out_dtype, residual=None, tm=1024, tn=512):
    m, k = x.shape
    n = w.shape[-1]
    assert m % tm == 0 and n % tn == 0 and w.shape[-2] == k
    in_specs = [pl.BlockSpec((tm, k), lambda j, i: (i, 0)),
                pl.BlockSpec((None, k, tn), lambda j, i: (layer, 0, j))]
    args = [x, w]
    if residual is not None:
        in_specs.append(pl.BlockSpec((tm, tn), lambda j, i: (i, j)))
        args.append(residual)
    return pl.pallas_call(
        functools.partial(_proj_kernel, has_res=residual is not None),
        grid=(n // tn, m // tm),
        in_specs=in_specs,
        out_specs=pl.BlockSpec((tm, tn), lambda j, i: (i, j)),
        out_shape=jax.ShapeDtypeStruct((m, n), out_dtype),
        scratch_shapes=[pltpu.VMEM((k, tn), BF16)],
        compiler_params=_params("arbitrary", "arbitrary"),
        name="proj",
    )(*args)


def _ffn_in_kernel(x_ref, wg_ref, wu_ref, cwg_ref, cwu_ref, bg_ref, bu_ref, o_ref,
                   wb_ref, hbuf_ref, *, tm, tn, ni, tiles_per_seq, chunk, mslab):
    s = pl.program_id(0)

    @pl.when(s == 0)
    def _():
        hbuf_ref[...] = jnp.zeros(hbuf_ref.shape, F32)

    @pl.when(s % ni == 0)
    def _():
        wb_ref[:, :tn] = wg_ref[...].astype(BF16)
        wb_ref[:, tn:] = wu_ref[...].astype(BF16)

    @pl.when((jnp.maximum(s - 1, 0) % ni) % tiles_per_seq == 0)
    def _():
        hbuf_ref[0:HALO, :] = jnp.zeros((HALO, 2 * tn), F32)

    cw = jnp.concatenate([cwg_ref[...], cwu_ref[...]], axis=1)
    bias = jnp.concatenate([bg_ref[...], bu_ref[...]], axis=1)
    w0, w1, w2 = cw[0:1], cw[1:2], cw[2:3]

    def conv_gate(r0, r1):
        for r in range(r0, r1, chunk):
            h0 = hbuf_ref[HALO + r:HALO + r + chunk, :]
            h1 = hbuf_ref[HALO - 1 + r:HALO - 1 + r + chunk, :]
            h2 = hbuf_ref[HALO - 2 + r:HALO - 2 + r + chunk, :]
            y = ((bias + w0 * h2) + w1 * h1) + w2 * h0
            gate, up = y[:, :tn], y[:, tn:]
            act = (gate * jax.nn.sigmoid(gate)) * up
            o_ref[r:r + chunk, :] = act.astype(o_ref.dtype)

    conv_gate(0, mslab)
    for c in range(tm // mslab):
        r = c * mslab
        if r + mslab < tm:
            conv_gate(r + mslab, r + 2 * mslab)
        else:
            hbuf_ref[0:HALO, :] = hbuf_ref[tm:tm + HALO, :]
        hbuf_ref[HALO + r:HALO + r + mslab, :] = jnp.dot(x_ref[r:r + mslab, :], wb_ref[...],
                                                         preferred_element_type=F32)


def ffn_in(x, w_in, conv_w, conv_b, layer, seq, tm=1024, tn=256, chunk=128, mslab=128):
    m, k = x.shape
    d_ff = w_in.shape[-1] // 2
    assert m % tm == 0 and seq % tm == 0 and d_ff % tn == 0
    assert tm % mslab == 0 and mslab % chunk == 0 and CONV_WIDTH - 1 <= HALO
    nj = d_ff // tn
    ni = m // tm
    last = nj * ni - 1
    conv_b3 = conv_b.reshape(conv_b.shape[0], 1, 2 * d_ff)

    def mm_col(s):
        return jnp.minimum(s, last) // ni

    def ep_col(s):
        return jnp.maximum(s - 1, 0) // ni

    return pl.pallas_call(
        functools.partial(_ffn_in_kernel, tm=tm, tn=tn, ni=ni, tiles_per_seq=seq // tm,
                          chunk=chunk, mslab=mslab),
        grid=(nj * ni + 1,),
        in_specs=[pl.BlockSpec((tm, k), lambda s: (jnp.minimum(s, last) % ni, 0)),
                  pl.BlockSpec((None, k, tn), lambda s: (layer, 0, mm_col(s))),
                  pl.BlockSpec((None, k, tn), lambda s: (layer, 0, mm_col(s) + nj)),
                  pl.BlockSpec((None, CONV_WIDTH, tn), lambda s: (layer, 0, ep_col(s))),
                  pl.BlockSpec((None, CONV_WIDTH, tn), lambda s: (layer, 0, ep_col(s) + nj)),
                  pl.BlockSpec((None, 1, tn), lambda s: (layer, 0, ep_col(s))),
                  pl.BlockSpec((None, 1, tn), lambda s: (layer, 0, ep_col(s) + nj))],
        out_specs=pl.BlockSpec((tm, tn), lambda s: (jnp.maximum(s - 1, 0) % ni, ep_col(s))),
        out_shape=jax.ShapeDtypeStruct((m, d_ff), BF16),
        scratch_shapes=[pltpu.VMEM((k, 2 * tn), BF16),
                        pltpu.VMEM((tm + HALO, 2 * tn), F32)],
        compiler_params=_params("arbitrary"),
        name="ffn_in",
    )(x, w_in, w_in, conv_w, conv_w, conv_b3, conv_b3)


def _ffn_down_kernel(x_ref, w_ref, r_ref, o_ref):
    @pl.when(pl.program_id(2) == 0)
    def _():
        o_ref[...] = r_ref[...]

    o_ref[...] += jnp.dot(x_ref[...], w_ref[...].astype(BF16), preferred_element_type=F32)


def ffn_down(x, w, layer, residual, tm=2048, tn=1024, tk=1024):
    m, k = x.shape
    n = w.shape[-1]
    assert m % tm == 0 and n % tn == 0 and k % tk == 0
    return pl.pallas_call(
        _ffn_down_kernel,
        grid=(m // tm, n // tn, k // tk),
        in_specs=[pl.BlockSpec((tm, tk), lambda i, j, kk: (i, kk)),
                  pl.BlockSpec((None, tk, tn), lambda i, j, kk: (layer, kk, j)),
                  pl.BlockSpec((tm, tn), lambda i, j, kk: (i, j))],
        out_specs=pl.BlockSpec((tm, tn), lambda i, j, kk: (i, j)),
        out_shape=jax.ShapeDtypeStruct((m, n), F32),
        compiler_params=_params("arbitrary", "arbitrary", "arbitrary"),
        name="ffn_down",
    )(x, w, residual)


def _sb_kernel(q_ref, k_ref, v_ref, tri_ref, o_ref, todo_ref, c0_ref, acc0_ref, z_ref, hi_ref,
               zs_ref, cum_ref, w_ref, sp0_ref, r_ref, c_ref, acc_ref, *, seq, tile, group, scale):
    t = tile
    t2 = 2 * t
    npairs = seq // t2
    d = SB_HEAD_DIM
    tri = tri_ref[...]
    row = lax.broadcasted_iota(jnp.int32, (t, t), 0)
    col = lax.broadcasted_iota(jnp.int32, (t, t), 1)
    strict = col < row
    sign = jnp.uint32(0x80000000)

    heads = list(range(group))

    def key_tile(ref, g, kb):
        return ref[pl.ds(pl.multiple_of(kb * t, t), t), g * d:(g + 1) * d]

    def logits(qs, gs, kbs):
        return [lax.dot_general(q, key_tile(k_ref, g, kb), (((1,), (1,)), ((), ())),
                                preferred_element_type=F32)
                for q, g, kb in zip(qs, gs, kbs)]

    def suffix_sums(z_raw, masked):
        out = []
        zs = [z * (scale * LOG2E) for z in z_raw]
        sps = []
        for z in zs:
            neg_abs = lax.bitcast_convert_type(lax.bitcast_convert_type(z, jnp.uint32) | sign, F32)
            sp = jnp.maximum(z, 0.0) + jnp.log2(1.0 + jnp.exp2(neg_abs))
            sps.append(jnp.where(strict, sp, 0.0) if masked else sp)
        for z, sp in zip(zs, sps):
            cum = jnp.dot(sp.astype(BF16), tri, preferred_element_type=F32)
            rowsum = cum[:, 0:1] + sp[:, 0:1]
            if masked:
                cum = jnp.where(strict, cum, -NEG_BIG)
            out.append((z - sp, cum, rowsum))
        return out

    def accumulate(parts, gs, kbs, cs, accs):
        ws = [jnp.exp2((p[0] - p[1]) - c).astype(BF16) for p, c in zip(parts, cs)]
        accs = [a + jnp.dot(w, key_tile(v_ref, g, kb), preferred_element_type=F32)
                for a, w, g, kb in zip(accs, ws, gs, kbs)]
        cs = [c + p[2] for c, p in zip(cs, parts)]
        return cs, accs

    def visit(qs, gs, kbs, masked, cs, accs):
        return accumulate(suffix_sums(logits(qs, gs, kbs), masked), gs, kbs, cs, accs)

    def diag_body(qp, carry):
        cnt, total = carry
        q0 = 2 * qp
        q1 = q0 + 1
        has_prev = qp >= 1
        kprev = jnp.maximum(q0 - 1, 0)
        rows = pl.ds(pl.multiple_of(qp * t2, t2), t2)
        qboth = [q_ref[rows, g * d:(g + 1) * d] for g in heads]
        qab = [q[:t] for q in qboth] + [q[t:] for q in qboth]
        zc = [jnp.zeros((t, 1), F32)] * (2 * group)
        za = [jnp.zeros((t, d), F32)] * (2 * group)
        cs, accs = visit(qab, heads + heads, [q0] * group + [q1] * group, True, zc, za)
        cs2, accs2 = visit(qab, heads + heads, [kprev] * group + [q0] * group, False, cs, accs)
        cmin = None
        for g in heads:
            ca = jnp.where(has_prev, cs2[g], cs[g])
            aa = jnp.where(has_prev, accs2[g], accs[g])
            cb, ab = cs2[group + g], accs2[group + g]
            acc = jnp.concatenate([aa, ab], axis=0)
            c0_ref[g, rows, :] = jnp.concatenate([ca, cb], axis=0)
            acc0_ref[rows, g * d:(g + 1) * d] = acc
            o_ref[rows, g * d:(g + 1) * d] = acc.astype(o_ref.dtype)
            m = jnp.minimum(ca, cb)
            cmin = m if cmin is None else jnp.minimum(cmin, m)
        more = has_prev & (jnp.min(cmin) < SB_DONE)

        @pl.when(more)
        def _():
            todo_ref[cnt] = qp

        more_i = more.astype(jnp.int32)
        return cnt + more_i, total + more_i * (2 * qp)

    todo_ref[0] = jnp.int32(1)
    cnt, total = lax.fori_loop(0, npairs, diag_body, (jnp.int32(0), jnp.int32(0)))

    first_tile_rows = lax.broadcasted_iota(jnp.int32, (t2, 1), 0) < t
    @pl.when(total > 0)
    def _():
        for buf in (z_ref, hi_ref, zs_ref, cum_ref, w_ref, sp0_ref, r_ref, c_ref, acc_ref):
            buf[...] = jnp.zeros(buf.shape, buf.dtype)

    def listed(li):
        return todo_ref[jnp.clip(li, 0, jnp.maximum(cnt - 1, 0))]

    def step(i, state):
        li, hist = state
        (q1, k1), (q2, k2), (q3, k3), (q4, k4), (q5, k5) = hist
        slot = i % 2

        rows5 = pl.ds(pl.multiple_of(q5 * t2, t2), t2)
        first5 = k5 == 2 * q5 - 1
        for g in heads:
            a_in = jnp.where(first5, acc0_ref[rows5, g * d:(g + 1) * d], acc_ref[g])
            a = a_in + jnp.dot(w_ref[g], key_tile(v_ref, g, k5), preferred_element_type=F32)
            acc_ref[g] = a
            o_ref[rows5, g * d:(g + 1) * d] = a.astype(o_ref.dtype)

        rows4 = pl.ds(pl.multiple_of(q4 * t2, t2), t2)
        first4 = k4 == 2 * q4 - 1
        seen = first_tile_rows & first4
        for g in heads:
            c_in = jnp.where(first4, c0_ref[g, rows4, :], c_ref[g])
            w = jnp.exp2((zs_ref[slot, g] - cum_ref[g]) - c_in)
            w_ref[g] = jnp.where(seen, 0.0, w).astype(BF16)
            c_ref[g] = c_in + jnp.where(seen, 0.0, r_ref[g])

        for g in heads:
            cum = jnp.dot(hi_ref[g], tri, preferred_element_type=F32)
            cum_ref[g] = cum
            r_ref[g] = cum[:, 0:1] + sp0_ref[g]

        for g in heads:
            z = z_ref[g] * (scale * LOG2E)
            neg_abs = lax.bitcast_convert_type(lax.bitcast_convert_type(z, jnp.uint32) | sign, F32)
            sp = jnp.maximum(z, 0.0) + jnp.log2(1.0 + jnp.exp2(neg_abs))
            hi_ref[g] = sp.astype(BF16)
            zs_ref[slot, g] = z - sp
            sp0_ref[g] = sp[:, 0:1]

        rows1 = pl.ds(pl.multiple_of(q1 * t2, t2), t2)
        for g in heads:
            z_ref[g] = lax.dot_general(q_ref[rows1, g * d:(g + 1) * d], key_tile(k_ref, g, k1),
                                       (((1,), (1,)), ((), ())), preferred_element_type=F32)

        wrap = k1 == 0
        done = wrap & (li + 1 >= cnt)
        li_n = jnp.where(wrap & ~done, li + 1, li)
        q_w = listed(li_n)
        qn = jnp.where(done, q1, jnp.where(wrap, q_w, q1))
        kn = jnp.where(done, k1, jnp.where(wrap, 2 * q_w - 1, k1 - 1))
        return li_n, ((qn, kn), (q1, k1), (q2, k2), (q3, k3), (q4, k4))

    qf = listed(0)
    steps = jnp.where(total > 0, total + 4, 0)
    lax.fori_loop(0, steps, step, (jnp.int32(0), ((qf, 2 * qf - 1),) * 5))


def sb_attention(qkv, batch, seq, tile=256, group=2):
    m = qkv.shape[0]
    heads = qkv.shape[1] // (3 * SB_HEAD_DIM)
    ng = heads // group
    d = SB_HEAD_DIM
    gw = group * d
    t2 = 2 * tile
    assert heads % group == 0 and seq % t2 == 0 and m == batch * seq
    idx = jnp.arange(tile)
    tri = (idx[:, None] > idx[None, :]).astype(BF16)
    return pl.pallas_call(
        functools.partial(_sb_kernel, seq=seq, tile=tile, group=group, scale=d ** -0.5),
        grid=(batch, ng),
        in_specs=[pl.BlockSpec((seq, gw), lambda b, h: (b, h)),
                  pl.BlockSpec((seq, gw), lambda b, h: (b, ng + h)),
                  pl.BlockSpec((seq, gw), lambda b, h: (b, 2 * ng + h)),
                  pl.BlockSpec((tile, tile), lambda b, h: (0, 0))],
        out_specs=pl.BlockSpec((seq, gw), lambda b, h: (b, h)),
        out_shape=jax.ShapeDtypeStruct((m, heads * d), BF16),
        scratch_shapes=[pltpu.SMEM((seq // t2,), jnp.int32),
                        pltpu.VMEM((group, seq, 1), F32),
                        pltpu.VMEM((seq, gw), F32),
                        pltpu.VMEM((group, t2, tile), F32),
                        pltpu.VMEM((group, t2, tile), BF16),
                        pltpu.VMEM((2, group, t2, tile), F32),
                        pltpu.VMEM((group, t2, tile), F32),
                        pltpu.VMEM((group, t2, tile), BF16),
                        pltpu.VMEM((group, t2, 1), F32),
                        pltpu.VMEM((group, t2, 1), F32),
                        pltpu.VMEM((group, t2, 1), F32),
                        pltpu.VMEM((group, t2, d), F32)],
        compiler_params=_params("arbitrary", "arbitrary"),
        name="sb_attention",
    )(qkv, qkv, qkv, tri)


def _swa_kernel(slopes_ref, sinks_ref, q_ref, k_ref, v_ref, o_ref, *, seq, scale):
    blk_sz = SWA_BLOCK
    kvh = pl.program_id(1)
    odd = (kvh % 2) == 1
    lane = lax.broadcasted_iota(jnp.int32, (2 * blk_sz, LANES), 1)
    low = lane < SWA_HEAD_DIM

    r_idx = lax.broadcasted_iota(jnp.int32, (blk_sz, blk_sz), 0)
    j_idx = lax.broadcasted_iota(jnp.int32, (blk_sz, blk_sz), 1)
    from_prev = j_idx > r_idx
    dist_f = jnp.where(from_prev, r_idx + blk_sz - j_idx, r_idx - j_idx).astype(F32)

    def place(band):
        x = band.astype(F32)
        x = jnp.where(odd, pltpu.roll(x, SWA_HEAD_DIM, 1), x)
        at0 = jnp.where(low, x, 0.0)
        at64 = pltpu.roll(at0, SWA_HEAD_DIM, 1)
        return at0.astype(BF16), at64.astype(BF16)

    def body(blk, carry):
        cur = pl.multiple_of(blk * blk_sz, blk_sz)
        prev = pl.multiple_of(jnp.maximum(blk - 1, 0) * blk_sz, blk_sz)
        kband = jnp.concatenate([k_ref[pl.ds(prev, blk_sz), :], k_ref[pl.ds(cur, blk_sz), :]], axis=0)
        vband = jnp.concatenate([v_ref[pl.ds(prev, blk_sz), :], v_ref[pl.ds(cur, blk_sz), :]], axis=0)
        k_at = place(kband)
        v_at = place(vband)
        masked = from_prev & (blk == 0)
        for c in range(SWA_GROUP // 2):
            qc = q_ref[pl.ds(cur, blk_sz), c * LANES:(c + 1) * LANES]
            out = None
            for half in range(2):
                head = kvh * SWA_GROUP + 2 * c + half
                s2 = lax.dot_general(qc, k_at[half], (((1,), (1,)), ((), ())),
                                     preferred_element_type=F32)
                s = jnp.where(from_prev, s2[:, :blk_sz], s2[:, blk_sz:])
                s = s * scale - slopes_ref[head] * dist_f
                s = jnp.where(masked, NEG_BIG, s)
                sink = sinks_ref[head]
                mx = jnp.maximum(jnp.max(s, axis=-1, keepdims=True), sink)
                p = jnp.exp(s - mx)
                denom = jnp.sum(p, axis=-1, keepdims=True) + jnp.exp(sink - mx)
                p2 = jnp.concatenate([jnp.where(from_prev, p, 0.0), jnp.where(from_prev, 0.0, p)],
                                     axis=1)
                o = jnp.dot(p2.astype(BF16), v_at[half], preferred_element_type=F32) / denom
                out = o if out is None else out + o
            o_ref[pl.ds(cur, blk_sz), c * LANES:(c + 1) * LANES] = out.astype(o_ref.dtype)
        return carry

    lax.fori_loop(0, seq // blk_sz, body, 0)


def swa_attention(qkv, slopes, sinks, batch, seq, heads):
    m = qkv.shape[0]
    kv_heads = heads // SWA_GROUP
    qd = heads * SWA_HEAD_DIM
    gw = SWA_GROUP * SWA_HEAD_DIM
    k_blk0 = qd // LANES
    v_blk0 = (qd + kv_heads * SWA_HEAD_DIM) // LANES
    smem = pl.BlockSpec(memory_space=pltpu.SMEM)
    return pl.pallas_call(
        functools.partial(_swa_kernel, seq=seq, scale=SWA_HEAD_DIM ** -0.5),
        grid=(batch, kv_heads),
        in_specs=[smem, smem,
                  pl.BlockSpec((seq, gw), lambda b, h: (b, h)),
                  pl.BlockSpec((seq, LANES), lambda b, h: (b, k_blk0 + h // 2)),
                  pl.BlockSpec((seq, LANES), lambda b, h: (b, v_blk0 + h // 2))],
        out_specs=pl.BlockSpec((seq, gw), lambda b, h: (b, h)),
        out_shape=jax.ShapeDtypeStruct((m, qd), BF16),
        compiler_params=_params("arbitrary", "arbitrary"),
        name="swa_attention",
    )(slopes, sinks, qkv, qkv, qkv)


def kernel(x, attn_norm, ffn_norm, sb_w_qkv, sb_w_o, swa_w_qkv, swa_w_o, swa_sinks,
           ffn_w_in, ffn_conv_w, ffn_conv_b, ffn_w_down, final_norm):
    batch, seq, d = x.shape
    depth = attn_norm.shape[0]
    swa_heads = swa_sinks.shape[-1]
    slopes = jnp.asarray(2.0 ** (-8.0 * np.arange(1, swa_heads + 1) / swa_heads), F32)
    h = x.reshape(batch * seq, d)
    for i in range(depth):
        xn = rmsnorm(h, attn_norm, i, BF16)
        j = i // 2
        if i % 2 == 0:
            qkv = proj(xn, sb_w_qkv, j, BF16)
            mix = sb_attention(qkv, batch, seq)
            h = proj(mix, sb_w_o, j, F32, residual=h)
        else:
            qkv = proj(xn, swa_w_qkv, j, BF16)
            mix = swa_attention(qkv, slopes, swa_sinks[j], batch, seq, swa_heads)
            h = proj(mix, swa_w_o, j, F32, residual=h)
        xn = rmsnorm(h, ffn_norm, i, BF16)
        act = ffn_in(xn, ffn_w_in, ffn_conv_w, ffn_conv_b, i, seq)
        h = ffn_down(act, ffn_w_down, i, h)
    out = rmsnorm(h, final_norm.reshape(1, d), 0, F32)
    return out.reshape(batch, seq, d)
```

```python
import functools

import jax
import jax.numpy as jnp
import numpy as np
from jax import lax
from jax.experimental import pallas as pl
from jax.experimental.pallas import tpu as pltpu

F32 = jnp.float32
BF16 = jnp.bfloat16

RMS_EPS = 1e-6
SB_HEAD_DIM = 128
SWA_HEAD_DIM = 64
SWA_GROUP = 8
SWA_BLOCK = 128
CONV_WIDTH = 3
NEG_BIG = -1e30
LOG2E = 1.4426950408889634
SB_DONE = 192.0

V7X_VMEM_BYTES = 64 * 1024 * 1024
VMEM_LIMIT = V7X_VMEM_BYTES * 7 // 8
HALO = 8
LANES = 128


def _params(*sem):
    return pltpu.CompilerParams(dimension_semantics=sem, vmem_limit_bytes=VMEM_LIMIT)


def _rmsnorm_kernel(x_ref, g_ref, o_ref):
    x = x_ref[...]
    ms = jnp.mean(x * x, axis=-1, keepdims=True)
    o_ref[...] = ((x * lax.rsqrt(ms + RMS_EPS)) * g_ref[...]).astype(o_ref.dtype)


def rmsnorm(x, gain, layer, out_dtype, tm=512):
    m, d = x.shape
    assert m % tm == 0
    gain3 = gain.reshape(-1, 1, d)
    return pl.pallas_call(
        _rmsnorm_kernel,
        grid=(m // tm,),
        in_specs=[pl.BlockSpec((tm, d), lambda i: (i, 0)),
                  pl.BlockSpec((None, 1, d), lambda i: (layer, 0, 0))],
        out_specs=pl.BlockSpec((tm, d), lambda i: (i, 0)),
        out_shape=jax.ShapeDtypeStruct((m, d), out_dtype),
        compiler_params=_params("arbitrary"),
        name="rmsnorm",
    )(x, gain3)


def _proj_kernel(x_ref, w_ref, *rest, has_res):
    if has_res:
        r_ref, o_ref, wb_ref = rest
    else:
        o_ref, wb_ref = rest

    @pl.when(pl.program_id(1) == 0)
    def _():
        wb_ref[...] = w_ref[...].astype(BF16)

    acc = jnp.dot(x_ref[...], wb_ref[...], preferred_element_type=F32)
    if has_res:
        acc = acc + r_ref[...]
    o_ref[...] = acc.astype(o_ref.dtype)


def proj(x, w, layer, out_dtype, residual=None, tm=1024, tn=512):
    m, k = x.shape
    n = w.shape[-1]
    assert m % tm == 0 and n % tn == 0 and w.shape[-2] == k
    in_specs = [pl.BlockSpec((tm, k), lambda j, i: (i, 0)),
                pl.BlockSpec((None, k, tn), lambda j, i: (layer, 0, j))]
    args = [x, w]
    if residual is not None:
        in_specs.append(pl.BlockSpec((tm, tn), lambda j, i: (i, j)))
        args.append(residual)
    return pl.pallas_call(
        functools.partial(_proj_kernel, has_res=residual is not None),
        grid=(n // tn, m // tm),
        in_specs=in_specs,
        out_specs=pl.BlockSpec((tm, tn), lambda j, i: (i, j)),
        out_shape=jax.ShapeDtypeStruct((m, n), out_dtype),
        scratch_shapes=[pltpu.VMEM((k, tn), BF16)],
        compiler_params=_params("arbitrary", "arbitrary"),
        name="proj",
    )(*args)


def _ffn_in_kernel(x_ref, wg_ref, wu_ref, cwg_ref, cwu_ref, bg_ref, bu_ref, o_ref,
                   wb_ref, hbuf_ref, *, tm, tn, ni, tiles_per_seq, chunk, mslab):
    s = pl.program_id(0)

    @pl.when(s == 0)
    def _():
        hbuf_ref[...] = jnp.zeros(hbuf_ref.shape, F32)

    @pl.when(s % ni == 0)
    def _():
        wb_ref[:, :tn] = wg_ref[...].astype(BF16)
        wb_ref[:, tn:] = wu_ref[...].astype(BF16)

    @pl.when((jnp.maximum(s - 1, 0) % ni) % tiles_per_seq == 0)
    def _():
        hbuf_ref[0:HALO, :] = jnp.zeros((HALO, 2 * tn), F32)

    cw = jnp.concatenate([cwg_ref[...], cwu_ref[...]], axis=1)
    bias = jnp.concatenate([bg_ref[...], bu_ref[...]], axis=1)
    w0, w1, w2 = cw[0:1], cw[1:2], cw[2:3]

    def conv_gate(r0, r1):
        for r in range(r0, r1, chunk):
            h0 = hbuf_ref[HALO + r:HALO + r + chunk, :]
            h1 = hbuf_ref[HALO - 1 + r:HALO - 1 + r + chunk, :]
            h2 = hbuf_ref[HALO - 2 + r:HALO - 2 + r + chunk, :]
            y = ((bias + w0 * h2) + w1 * h1) + w2 * h0
            gate, up = y[:, :tn], y[:, tn:]
            act = (gate * jax.nn.sigmoid(gate)) * up
            o_ref[r:r + chunk, :] = act.astype(o_ref.dtype)

    conv_gate(0, mslab)
    for c in range(tm // mslab):
        r = c * mslab
        if r + mslab < tm:
            conv_gate(r + mslab, r + 2 * mslab)
        else:
            hbuf_ref[0:HALO, :] = hbuf_ref[tm:tm + HALO, :]
        hbuf_ref[HALO + r:HALO + r + mslab, :] = jnp.dot(x_ref[r:r + mslab, :], wb_ref[...],
                                                         preferred_element_type=F32)


def ffn_in(x, w_in, conv_w, conv_b, layer, seq, tm=1024, tn=256, chunk=128, mslab=128):
    m, k = x.shape
    d_ff = w_in.shape[-1] // 2
    assert m % tm == 0 and seq % tm == 0 and d_ff % tn == 0
    assert tm % mslab == 0 and mslab % chunk == 0 and CONV_WIDTH - 1 <= HALO
    nj = d_ff // tn
    ni = m // tm
    last = nj * ni - 1
    conv_b3 = conv_b.reshape(conv_b.shape[0], 1, 2 * d_ff)

    def mm_col(s):
        return jnp.minimum(s, last) // ni

    def ep_col(s):
        return jnp.maximum(s - 1, 0) // ni

    return pl.pallas_call(
        functools.partial(_ffn_in_kernel, tm=tm, tn=tn, ni=ni, tiles_per_seq=seq // tm,
                          chunk=chunk, mslab=mslab),
        grid=(nj * ni + 1,),
        in_specs=[pl.BlockSpec((tm, k), lambda s: (jnp.minimum(s, last) % ni, 0)),
                  pl.BlockSpec((None, k, tn), lambda s: (layer, 0, mm_col(s))),
                  pl.BlockSpec((None, k, tn), lambda s: (layer, 0, mm_col(s) + nj)),
                  pl.BlockSpec((None, CONV_WIDTH, tn), lambda s: (layer, 0, ep_col(s))),
                  pl.BlockSpec((None, CONV_WIDTH, tn), lambda s: (layer, 0, ep_col(s) + nj)),
                  pl.BlockSpec((None, 1, tn), lambda s: (layer, 0, ep_col(s))),
                  pl.BlockSpec((None, 1, tn), lambda s: (layer, 0, ep_col(s) + nj))],
        out_specs=pl.BlockSpec((tm, tn), lambda s: (jnp.maximum(s - 1, 0) % ni, ep_col(s))),
        out_shape=jax.ShapeDtypeStruct((m, d_ff), BF16),
        scratch_shapes=[pltpu.VMEM((k, 2 * tn), BF16),
                        pltpu.VMEM((tm + HALO, 2 * tn), F32)],
        compiler_params=_params("arbitrary"),
        name="ffn_in",
    )(x, w_in, w_in, conv_w, conv_w, conv_b3, conv_b3)


def _ffn_down_kernel(x_ref, w_ref, r_ref, o_ref):
    @pl.when(pl.program_id(2) == 0)
    def _():
        o_ref[...] = r_ref[...]

    o_ref[...] += jnp.dot(x_ref[...], w_ref[...].astype(BF16), preferred_element_type=F32)


def ffn_down(x, w, layer, residual, tm=2048, tn=1024, tk=1024):
    m, k = x.shape
    n = w.shape[-1]
    assert m % tm == 0 and n % tn == 0 and k % tk == 0
    return pl.pallas_call(
        _ffn_down_kernel,
        grid=(m // tm, n // tn, k // tk),
        in_specs=[pl.BlockSpec((tm, tk), lambda i, j, kk: (i, kk)),
                  pl.BlockSpec((None, tk, tn), lambda i, j, kk: (layer, kk, j)),
                  pl.BlockSpec((tm, tn), lambda i, j, kk: (i, j))],
        out_specs=pl.BlockSpec((tm, tn), lambda i, j, kk: (i, j)),
        out_shape=jax.ShapeDtypeStruct((m, n), F32),
        compiler_params=_params("arbitrary", "arbitrary", "arbitrary"),
        name="ffn_down",
    )(x, w, residual)


def _sb_kernel(q_ref, k_ref, v_ref, tri_ref, o_ref, todo_ref, c0_ref, acc0_ref, z_ref, hi_ref,
               zs_ref, cum_ref, w_ref, sp0_ref, r_ref, c_ref, acc_ref, *, seq, tile, group, scale):
    t = tile
    t2 = 2 * t
    npairs = seq // t2
    d = SB_HEAD_DIM
    tri = tri_ref[...]
    row = lax.broadcasted_iota(jnp.int32, (t, t), 0)
    col = lax.broadcasted_iota(jnp.int32, (t, t), 1)
    strict = col < row
    sign = jnp.uint32(0x80000000)

    heads = list(range(group))

    def key_tile(ref, g, kb):
        return ref[pl.ds(pl.multiple_of(kb * t, t), t), g * d:(g + 1) * d]

    def logits(qs, gs, kbs):
        return [lax.dot_general(q, key_tile(k_ref, g, kb), (((1,), (1,)), ((), ())),
                                preferred_element_type=F32)
                for q, g, kb in zip(qs, gs, kbs)]

    def suffix_sums(z_raw, masked):
        out = []
        zs = [z * (scale * LOG2E) for z in z_raw]
        sps = []
        for z in zs:
            neg_abs = lax.bitcast_convert_type(lax.bitcast_convert_type(z, jnp.uint32) | sign, F32)
            sp = jnp.maximum(z, 0.0) + jnp.log2(1.0 + jnp.exp2(neg_abs))
            sps.append(jnp.where(strict, sp, 0.0) if masked else sp)
        for z, sp in zip(zs, sps):
            cum = jnp.dot(sp.astype(BF16), tri, preferred_element_type=F32)
            rowsum = cum[:, 0:1] + sp[:, 0:1]
            if masked:
                cum = jnp.where(strict, cum, -NEG_BIG)
            out.append((z - sp, cum, rowsum))
        return out

    def accumulate(parts, gs, kbs, cs, accs):
        ws = [jnp.exp2((p[0] - p[1]) - c).astype(BF16) for p, c in zip(parts, cs)]
        accs = [a + jnp.dot(w, key_tile(v_ref, g, kb), preferred_element_type=F32)
                for a, w, g, kb in zip(accs, ws, gs, kbs)]
        cs = [c + p[2] for c, p in zip(cs, parts)]
        return cs, accs

    def visit(qs, gs, kbs, masked, cs, accs):
        return accumulate(suffix_sums(logits(qs, gs, kbs), masked), gs, kbs, cs, accs)

    def diag_body(qp, carry):
        cnt, total = carry
        q0 = 2 * qp
        q1 = q0 + 1
        has_prev = qp >= 1
        kprev = jnp.maximum(q0 - 1, 0)
        rows = pl.ds(pl.multiple_of(qp * t2, t2), t2)
        qboth = [q_ref[rows, g * d:(g + 1) * d] for g in heads]
        qab = [q[:t] for q in qboth] + [q[t:] for q in qboth]
        zc = [jnp.zeros((t, 1), F32)] * (2 * group)
        za = [jnp.zeros((t, d), F32)] * (2 * group)
        cs, accs = visit(qab, heads + heads, [q0] * group + [q1] * group, True, zc, za)
        cs2, accs2 = visit(qab, heads + heads, [kprev] * group + [q0] * group, False, cs, accs)
        cmin = None
        for g in heads:
            ca = jnp.where(has_prev, cs2[g], cs[g])
            aa = jnp.where(has_prev, accs2[g], accs[g])
            cb, ab = cs2[group + g], accs2[group + g]
            acc = jnp.concatenate([aa, ab], axis=0)
            c0_ref[g, rows, :] = jnp.concatenate([ca, cb], axis=0)
            acc0_ref[rows, g * d:(g + 1) * d] = acc
            o_ref[rows, g * d:(g + 1) * d] = acc.astype(o_ref.dtype)
            m = jnp.minimum(ca, cb)
            cmin = m if cmin is None else jnp.minimum(cmin, m)
        more = has_prev & (jnp.min(cmin) < SB_DONE)

        @pl.when(more)
        def _():
            todo_ref[cnt] = qp

        more_i = more.astype(jnp.int32)
        return cnt + more_i, total + more_i * (2 * qp)

    todo_ref[0] = jnp.int32(1)
    cnt, total = lax.fori_loop(0, npairs, diag_body, (jnp.int32(0), jnp.int32(0)))

    first_tile_rows = lax.broadcasted_iota(jnp.int32, (t2, 1), 0) < t
    @pl.when(total > 0)
    def _():
        for buf in (z_ref, hi_ref, zs_ref, cum_ref, w_ref, sp0_ref, r_ref, c_ref, acc_ref):
            buf[...] = jnp.zeros(buf.shape, buf.dtype)

    def listed(li):
        return todo_ref[jnp.clip(li, 0, jnp.maximum(cnt - 1, 0))]

    def step(i, state):
        li, hist = state
        (q1, k1), (q2, k2), (q3, k3), (q4, k4), (q5, k5) = hist
        slot = i % 2

        rows5 = pl.ds(pl.multiple_of(q5 * t2, t2), t2)
        first5 = k5 == 2 * q5 - 1
        for g in heads:
            a_in = jnp.where(first5, acc0_ref[rows5, g * d:(g + 1) * d], acc_ref[g])
            a = a_in + jnp.dot(w_ref[g], key_tile(v_ref, g, k5), preferred_element_type=F32)
            acc_ref[g] = a
            o_ref[rows5, g * d:(g + 1) * d] = a.astype(o_ref.dtype)

        rows4 = pl.ds(pl.multiple_of(q4 * t2, t2), t2)
        first4 = k4 == 2 * q4 - 1
        seen = first_tile_rows & first4
        for g in heads:
            c_in = jnp.where(first4, c0_ref[g, rows4, :], c_ref[g])
            w = jnp.exp2((zs_ref[slot, g] - cum_ref[g]) - c_in)
            w_ref[g] = jnp.where(seen, 0.0, w).astype(BF16)
            c_ref[g] = c_in + jnp.where(seen, 0.0, r_ref[g])

        for g in heads:
            cum = jnp.dot(hi_ref[g], tri, preferred_element_type=F32)
            cum_ref[g] = cum
            r_ref[g] = cum[:, 0:1] + sp0_ref[g]

        for g in heads:
            z = z_ref[g] * (scale * LOG2E)
            neg_abs = lax.bitcast_convert_type(lax.bitcast_convert_type(z, jnp.uint32) | sign, F32)
            sp = jnp.maximum(z, 0.0) + jnp.log2(1.0 + jnp.exp2(neg_abs))
            hi_ref[g] = sp.astype(BF16)
            zs_ref[slot, g] = z - sp
            sp0_ref[g] = sp[:, 0:1]

        rows1 = pl.ds(pl.multiple_of(q1 * t2, t2), t2)
        for g in heads:
            z_ref[g] = lax.dot_general(q_ref[rows1, g * d:(g + 1) * d], key_tile(k_ref, g, k1),
                                       (((1,), (1,)), ((), ())), preferred_element_type=F32)

        wrap = k1 == 0
        done = wrap & (li + 1 >= cnt)
        li_n = jnp.where(wrap & ~done, li + 1, li)
        q_w = listed(li_n)
        qn = jnp.where(done, q1, jnp.where(wrap, q_w, q1))
        kn = jnp.where(done, k1, jnp.where(wrap, 2 * q_w - 1, k1 - 1))
        return li_n, ((qn, kn), (q1, k1), (q2, k2), (q3, k3), (q4, k4))

    qf = listed(0)
    steps = jnp.where(total > 0, total + 4, 0)
    lax.fori_loop(0, steps, step, (jnp.int32(0), ((qf, 2 * qf - 1),) * 5))


def sb_attention(qkv, batch, seq, tile=256, group=2):
    m = qkv.shape[0]
    heads = qkv.shape[1] // (3 * SB_HEAD_DIM)
    ng = heads // group
    d = SB_HEAD_DIM
    gw = group * d
    t2 = 2 * tile
    assert heads % group == 0 and seq % t2 == 0 and m == batch * seq
    idx = jnp.arange(tile)
    tri = (idx[:, None] > idx[None, :]).astype(BF16)
    return pl.pallas_call(
        functools.partial(_sb_kernel, seq=seq, tile=tile, group=group, scale=d ** -0.5),
        grid=(batch, ng),
        in_specs=[pl.BlockSpec((seq, gw), lambda b, h: (b, h)),
                  pl.BlockSpec((seq, gw), lambda b, h: (b, ng + h)),
                  pl.BlockSpec((seq, gw), lambda b, h: (b, 2 * ng + h)),
                  pl.BlockSpec((tile, tile), lambda b, h: (0, 0))],
        out_specs=pl.BlockSpec((seq, gw), lambda b, h: (b, h)),
        out_shape=jax.ShapeDtypeStruct((m, heads * d), BF16),
        scratch_shapes=[pltpu.SMEM((seq // t2,), jnp.int32),
                        pltpu.VMEM((group, seq, 1), F32),
                        pltpu.VMEM((seq, gw), F32),
                        pltpu.VMEM((group, t2, tile), F32),
                        pltpu.VMEM((group, t2, tile), BF16),
                        pltpu.VMEM((2, group, t2, tile), F32),
                        pltpu.VMEM((group, t2, tile), F32),
                        pltpu.VMEM((group, t2, tile), BF16),
                        pltpu.VMEM((group, t2, 1), F32),
                        pltpu.VMEM((group, t2, 1), F32),
                        pltpu.VMEM((group, t2, 1), F32),
                        pltpu.VMEM((group, t2, d), F32)],
        compiler_params=_params("arbitrary", "arbitrary"),
        name="sb_attention",
    )(qkv, qkv, qkv, tri)


def _swa_kernel(slopes_ref, sinks_ref, q_ref, k_ref, v_ref, o_ref, *, seq, scale):
    blk_sz = SWA_BLOCK
    kvh = pl.program_id(1)
    odd = (kvh % 2) == 1
    lane = lax.broadcasted_iota(jnp.int32, (2 * blk_sz, LANES), 1)
    low = lane < SWA_HEAD_DIM

    r_idx = lax.broadcasted_iota(jnp.int32, (blk_sz, blk_sz), 0)
    j_idx = lax.broadcasted_iota(jnp.int32, (blk_sz, blk_sz), 1)
    from_prev = j_idx > r_idx
    dist_f = jnp.where(from_prev, r_idx + blk_sz - j_idx, r_idx - j_idx).astype(F32)

    def place(band):
        x = band.astype(F32)
        x = jnp.where(odd, pltpu.roll(x, SWA_HEAD_DIM, 1), x)
        at0 = jnp.where(low, x, 0.0)
        at64 = pltpu.roll(at0, SWA_HEAD_DIM, 1)
        return at0.astype(BF16), at64.astype(BF16)

    def body(blk, carry):
        cur = pl.multiple_of(blk * blk_sz, blk_sz)
        prev = pl.multiple_of(jnp.maximum(blk - 1, 0) * blk_sz, blk_sz)
        kband = jnp.concatenate([k_ref[pl.ds(prev, blk_sz), :], k_ref[pl.ds(cur, blk_sz), :]], axis=0)
        vband = jnp.concatenate([v_ref[pl.ds(prev, blk_sz), :], v_ref[pl.ds(cur, blk_sz), :]], axis=0)
        k_at = place(kband)
        v_at = place(vband)
        masked = from_prev & (blk == 0)
        group = range(SWA_GROUP)
        qcs = [q_ref[pl.ds(cur, blk_sz), c * LANES:(c + 1) * LANES] for c in range(SWA_GROUP // 2)]
        s2s = [lax.dot_general(qcs[g // 2], k_at[g % 2], (((1,), (1,)), ((), ())),
                               preferred_element_type=F32) for g in group]
        p2s, denoms = [], []
        for g in group:
            head = kvh * SWA_GROUP + g
            s = jnp.where(from_prev, s2s[g][:, :blk_sz], s2s[g][:, blk_sz:])
            s = s * scale - slopes_ref[head] * dist_f
            s = jnp.where(masked, NEG_BIG, s)
            sink = sinks_ref[head]
            mx = jnp.maximum(jnp.max(s, axis=-1, keepdims=True), sink)
            p = jnp.exp(s - mx)
            denoms.append(jnp.sum(p, axis=-1, keepdims=True) + jnp.exp(sink - mx))
            p2s.append(jnp.concatenate([jnp.where(from_prev, p, 0.0), jnp.where(from_prev, 0.0, p)],
                                       axis=1).astype(BF16))
        os = [jnp.dot(p2s[g], v_at[g % 2], preferred_element_type=F32) / denoms[g] for g in group]
        for c in range(SWA_GROUP // 2):
            o_ref[pl.ds(cur, blk_sz), c * LANES:(c + 1) * LANES] = (
                os[2 * c] + os[2 * c + 1]).astype(o_ref.dtype)
        return carry

    lax.fori_loop(0, seq // blk_sz, body, 0)


def swa_attention(qkv, slopes, sinks, batch, seq, heads):
    m = qkv.shape[0]
    kv_heads = heads // SWA_GROUP
    qd = heads * SWA_HEAD_DIM
    gw = SWA_GROUP * SWA_HEAD_DIM
    k_blk0 = qd // LANES
    v_blk0 = (qd + kv_heads * SWA_HEAD_DIM) // LANES
    smem = pl.BlockSpec(memory_space=pltpu.SMEM)
    return pl.pallas_call(
        functools.partial(_swa_kernel, seq=seq, scale=SWA_HEAD_DIM ** -0.5),
        grid=(batch, kv_heads),
        in_specs=[smem, smem,
                  pl.BlockSpec((seq, gw), lambda b, h: (b, h)),
                  pl.BlockSpec((seq, LANES), lambda b, h: (b, k_blk0 + h // 2)),
                  pl.BlockSpec((seq, LANES), lambda b, h: (b, v_blk0 + h // 2))],
        out_specs=pl.BlockSpec((seq, gw), lambda b, h: (b, h)),
        out_shape=jax.ShapeDtypeStruct((m, qd), BF16),
        compiler_params=_params("arbitrary", "arbitrary"),
        name="swa_attention",
    )(slopes, sinks, qkv, qkv, qkv)


def kernel(x, attn_norm, ffn_norm, sb_w_qkv, sb_w_o, swa_w_qkv, swa_w_o, swa_sinks,
           ffn_w_in, ffn_conv_w, ffn_conv_b, ffn_w_down, final_norm):
    batch, seq, d = x.shape
    depth = attn_norm.shape[0]
    swa_heads = swa_sinks.shape[-1]
    slopes = jnp.asarray(2.0 ** (-8.0 * np.arange(1, swa_heads + 1) / swa_heads), F32)
    h = x.reshape(batch * seq, d)
    for i in range(depth):
        xn = rmsnorm(h, attn_norm, i, BF16)
        j = i // 2
        if i % 2 == 0:
            qkv = proj(xn, sb_w_qkv, j, BF16)
            mix = sb_attention(qkv, batch, seq)
            h = proj(mix, sb_w_o, j, F32, residual=h)
        else:
            qkv = proj(xn, swa_w_qkv, j, BF16)
            mix = swa_attention(qkv, slopes, swa_sinks[j], batch, seq, swa_heads)
            h = proj(mix, swa_w_o, j, F32, residual=h)
        xn = rmsnorm(h, ffn_norm, i, BF16)
        act = ffn_in(xn, ffn_w_in, ffn_conv_w, ffn_conv_b, i, seq)
        h = ffn_down(act, ffn_w_down, i, h)
    out = rmsnorm(h, final_norm.reshape(1, d), 0, F32)
    return out.reshape(batch, seq, d)
```
